```python
import math
import jax, jax.numpy as jnp
from jax import lax
import numpy as np

D_MODEL = 1024
BATCH = 2
SEQ = 8192
DEPTH = 4

ATT_HEADS = 8
ATT_HEAD_DIM = 128
ATT_WIDTH = ATT_HEADS * ATT_HEAD_DIM
IDX_HEADS = 4
IDX_HEAD_DIM = 64
TOPK_MAX = 256
Q_BLOCK = 128
ROPE_THETA = 500000.0
ATT_ROT = ATT_HEAD_DIM // 4
IDX_ROT = IDX_HEAD_DIM // 4
RWKV_HEAD = 64
RWKV_WIDTH = D_MODEL
RWKV_HEADS = RWKV_WIDTH // RWKV_HEAD
DECAY_LORA = 64
AAA_LORA = 64
GATE_LORA = 128
RWKV_GN_EPS = 64e-5
SSM_INNER = 2 * D_MODEL
SSM_HEAD_DIM = 64
SSM_HEADS = SSM_INNER // SSM_HEAD_DIM
SSM_GROUPS = 2
SSM_STATE = 128
SSM_CONV = 4
SSM_CHUNK = 128
SSM_CONV_CH = SSM_INNER + 2 * SSM_GROUPS * SSM_STATE
D_FF = 4 * D_MODEL
N_BRANCH = 3
NORM_EPS = 1e-6
A_COLS = 3 * ATT_WIDTH + IDX_HEADS * IDX_HEAD_DIM + IDX_HEAD_DIM + IDX_HEADS
B_COLS = 3 * RWKV_WIDTH + DECAY_LORA + AAA_LORA + GATE_LORA
C_COLS = SSM_INNER + SSM_CONV_CH + SSM_HEADS
GATE_COLS = N_BRANCH * D_MODEL
IN_COLS = A_COLS + B_COLS + C_COLS + GATE_COLS

kernel_name = "hybrid_dsa_rwkv7_mamba2_gated_block"


def split_cols(a, sizes):
    offs = np.cumsum(sizes)[:-1].tolist()
    return jnp.split(a, offs, axis=-1)


def rms_norm(x, g):
    xf = x.astype(jnp.float32)
    y = xf * lax.rsqrt(jnp.mean(xf * xf, axis=-1, keepdims=True) + NORM_EPS)
    return (y * g.astype(jnp.float32)).astype(x.dtype)


def rope_partial(x, pos, rot):
    half = rot // 2
    inv = ROPE_THETA ** (-jnp.arange(half, dtype=jnp.float32) * 2.0 / rot)
    ang = pos.astype(jnp.float32)[:, None] * inv[None, :]
    cos = jnp.cos(ang)[:, None, :]
    sin = jnp.sin(ang)[:, None, :]
    xr = x[..., :rot].astype(jnp.float32)
    x1, x2 = xr[..., :half], xr[..., half:]
    rotated = jnp.concatenate([x1 * cos - x2 * sin, x2 * cos + x1 * sin], axis=-1)
    return jnp.concatenate([rotated.astype(x.dtype), x[..., rot:]], axis=-1)


def dsa_attention(q, k, v, q_idx, k_idx, w_idx):
    bsz, seq = q.shape[:2]
    topk = min(TOPK_MAX, seq // 4)
    nblk = seq // Q_BLOCK
    att_scale = ATT_HEAD_DIM ** -0.5
    idx_scale = IDX_HEAD_DIM ** -0.5
    key_pos = jnp.arange(seq)
    k_idx_f = k_idx.astype(jnp.float32)

    def blockify(a):
        return a.reshape(bsz, nblk, Q_BLOCK, *a.shape[2:]).swapaxes(0, 1)

    def one_block(args):
        qb, qib, wib, start = args
        qpos = start + jnp.arange(Q_BLOCK)
        dots = jnp.einsum('bqhd,bsd->bqhs', qib.astype(jnp.float32), k_idx_f) * idx_scale
        score = jnp.einsum('bqhs,bqh->bqs', jax.nn.relu(dots), wib.astype(jnp.float32))
        causal = key_pos[None, :] <= qpos[:, None]
        score = jnp.where(causal[None], score, -jnp.inf)
        _, sel = lax.top_k(score, topk)
        kg = jax.vmap(lambda kb, ib: kb[ib])(k, sel)
        vg = jax.vmap(lambda vb, ib: vb[ib])(v, sel)
        logits = jnp.einsum('bqhd,bqkhd->bqhk', qb.astype(jnp.float32), kg.astype(jnp.float32)) * att_scale
        valid = (sel <= qpos[None, :, None])[:, :, None, :]
        probs = jax.nn.softmax(jnp.where(valid, logits, -jnp.inf), axis=-1)
        return jnp.einsum('bqhk,bqkhd->bqhd', probs, vg.astype(jnp.float32)).astype(qb.dtype)

    starts = jnp.arange(nblk) * Q_BLOCK
    out = lax.map(one_block, (blockify(q), blockify(q_idx), blockify(w_idx), starts))
    return out.swapaxes(0, 1).reshape(bsz, seq, *q.shape[2:])


def token_shift(p, mu):
    prev = jnp.pad(p, ((0, 0), (1, 0), (0, 0)))[:, :-1]
    return p + (prev - p) * mu


def rwkv7_time_mix(p, mu, w0, w2, a0, a2, g2, k_k, k_a, r_k, lnx_g, lnx_b):
    bsz, seq, _ = p.shape
    p = token_shift(p.astype(jnp.float32), mu)
    r, k, v, wl, al, gl = split_cols(p, [RWKV_WIDTH] * 3 + [DECAY_LORA, AAA_LORA, GATE_LORA])
    w = -jax.nn.softplus(-(w0 + jnp.tanh(wl) @ w2)) - 0.5
    decay = jnp.exp(-jnp.exp(w))
    a = jax.nn.sigmoid(a0 + al @ a2)
    g = jax.nn.sigmoid(gl) @ g2

    def heads(t):
        return t.reshape(bsz, seq, RWKV_HEADS, RWKV_HEAD).astype(jnp.float32)

    kk = heads(k * k_k)
    kk = kk / jnp.maximum(jnp.sqrt(jnp.sum(kk * kk, axis=-1, keepdims=True)), 1e-12)
    k = k * (1.0 + (a - 1.0) * k_a)
    r_h, k_h, v_h, a_h, w_h = heads(r), heads(k), heads(v), heads(a), heads(decay)
    a_vec = -kk
    b_vec = kk * a_h

    def step(S, inp):
        r_t, w_t, k_t, v_t, a_t, b_t = inp
        sa = jnp.einsum('bhvk,bhk->bhv', S, a_t)
        S = S * w_t[:, :, None, :] + sa[..., None] * b_t[:, :, None, :] + v_t[..., None] * k_t[:, :, None, :]
        return S, jnp.einsum('bhvk,bhk->bhv', S, r_t)

    xs = tuple(t.swapaxes(0, 1) for t in (r_h, w_h, k_h, v_h, a_vec, b_vec))
    S0 = jnp.zeros((bsz, RWKV_HEADS, RWKV_HEAD, RWKV_HEAD), jnp.float32)
    _, y = lax.scan(step, S0, xs)
    y = y.swapaxes(0, 1)
    mean = jnp.mean(y, axis=-1, keepdims=True)
    var = jnp.mean(jnp.square(y - mean), axis=-1, keepdims=True)
    y = ((y - mean) * lax.rsqrt(var + RWKV_GN_EPS)).reshape(bsz, seq, RWKV_WIDTH) * lnx_g + lnx_b
    bonus = (jnp.sum(r_h * k_h * r_k, axis=-1, keepdims=True) * v_h).reshape(bsz, seq, RWKV_WIDTH)
    return ((y + bonus) * g).astype(p.dtype)


def ssd_chunked(x, a, b, c):
    bsz, seq, nh, hp = x.shape
    L = SSM_CHUNK
    nc = seq // L
    G = b.shape[2]
    hg = nh // G
    x = x.reshape(bsz, nc, L, G, hg, hp)
    a = a.reshape(bsz, nc, L, G, hg)
    b = b.reshape(bsz, nc, L, G, SSM_STATE)
    c = c.reshape(bsz, nc, L, G, SSM_STATE)
    a_cum = jnp.cumsum(a, axis=2)
    seg = a_cum[:, :, :, None] - a_cum[:, :, None, :]
    tri = jnp.tril(jnp.ones((L, L), dtype=bool))[None, None, :, :, None, None]
    decay_in = jnp.exp(jnp.where(tri, seg, -jnp.inf))
    cb = jnp.einsum('bclgn,bcsgn->bclsg', c, b)
    y_diag = jnp.einsum('bclsg,bclsgh,bcsghp->bclghp', cb, decay_in, x)
    decay_to_end = jnp.exp(a_cum[:, :, -1:] - a_cum)
    states = jnp.einsum('bclgn,bclgh,bclghp->bcghpn', b, decay_to_end, x)
    chunk_decay = jnp.exp(a_cum[:, :, -1])

    def carry_fn(h, inp):
        st, dec = inp
        return h * dec[..., None, None] + st, h

    h0 = jnp.zeros((bsz, G, hg, hp, SSM_STATE), jnp.float32)
    _, h_in = lax.scan(carry_fn, h0, (states.swapaxes(0, 1), chunk_decay.swapaxes(0, 1)))
    h_in = h_in.swapaxes(0, 1)
    y_off = jnp.einsum('bclgn,bcghpn,bclgh->bclghp', c, h_in, jnp.exp(a_cum))
    return (y_diag + y_off).reshape(bsz, seq, nh, hp)


def mamba2_mixer(p, conv_w, conv_b, dt_bias, a_log, d_skip, norm_g):
    bsz, seq, _ = p.shape
    z, xbc, dt = split_cols(p, [SSM_INNER, SSM_CONV_CH, SSM_HEADS])
    xbc = lax.conv_general_dilated(xbc, conv_w[:, None, :].astype(xbc.dtype), window_strides=(1,),
                                   padding=[(SSM_CONV - 1, 0)], dimension_numbers=('NWC', 'WIO', 'NWC'),
                                   feature_group_count=SSM_CONV_CH) + conv_b
    xbc = jax.nn.silu(xbc.astype(jnp.float32))
    xs, bm, cm = split_cols(xbc, [SSM_INNER, SSM_GROUPS * SSM_STATE, SSM_GROUPS * SSM_STATE])
    dt = jax.nn.softplus(dt.astype(jnp.float32) + dt_bias)
    A = -jnp.exp(a_log.astype(jnp.float32))
    xh = xs.reshape(bsz, seq, SSM_HEADS, SSM_HEAD_DIM)
    y = ssd_chunked(xh * dt[..., None], dt * A,
                    bm.reshape(bsz, seq, SSM_GROUPS, SSM_STATE), cm.reshape(bsz, seq, SSM_GROUPS, SSM_STATE))
    y = y + xh * d_skip[:, None]
    y = y.reshape(bsz, seq, SSM_INNER) * jax.nn.silu(z.astype(jnp.float32))
    yg = y.reshape(bsz, seq, SSM_GROUPS, SSM_INNER // SSM_GROUPS)
    yg = yg * lax.rsqrt(jnp.mean(yg * yg, axis=-1, keepdims=True) + NORM_EPS)
    return (yg.reshape(bsz, seq, SSM_INNER) * norm_g).astype(p.dtype)


def hybrid_layer(x, pre_mix_g, w_in, w_o_att, rwkv_mu, rwkv_w0, rwkv_w2, rwkv_a0, rwkv_a2, rwkv_g2,
                 rwkv_k_k, rwkv_k_a, rwkv_r_k, rwkv_lnx_g, rwkv_lnx_b, w_o_rwkv, ssm_conv_w, ssm_conv_b,
                 ssm_dt_bias, ssm_a_log, ssm_d, ssm_norm_g, w_o_ssm, w_out, post_mix_g, pre_ffn_g,
                 w_ff1, w_ff2, post_ffn_g):
    bsz, seq, _ = x.shape
    pos = jnp.arange(seq)
    h = rms_norm(x, pre_mix_g)
    proj = h @ w_in
    pa, pb, pc, pg = split_cols(proj, [A_COLS, B_COLS, C_COLS, GATE_COLS])
    q, k, v, qi, ki, wi = split_cols(pa, [ATT_WIDTH] * 3 + [IDX_HEADS * IDX_HEAD_DIM, IDX_HEAD_DIM, IDX_HEADS])
    q = rope_partial(q.reshape(bsz, seq, ATT_HEADS, ATT_HEAD_DIM), pos, ATT_ROT)
    k = rope_partial(k.reshape(bsz, seq, ATT_HEADS, ATT_HEAD_DIM), pos, ATT_ROT)
    v = v.reshape(bsz, seq, ATT_HEADS, ATT_HEAD_DIM)
    qi = rope_partial(qi.reshape(bsz, seq, IDX_HEADS, IDX_HEAD_DIM), pos, IDX_ROT)
    ki = rope_partial(ki.reshape(bsz, seq, 1, IDX_HEAD_DIM), pos, IDX_ROT)[:, :, 0]
    wi = wi * (IDX_HEADS ** -0.5)
    y_a = dsa_attention(q, k, v, qi, ki, wi).reshape(bsz, seq, ATT_WIDTH) @ w_o_att
    y_b = rwkv7_time_mix(pb, rwkv_mu, rwkv_w0, rwkv_w2, rwkv_a0, rwkv_a2, rwkv_g2, rwkv_k_k, rwkv_k_a,
                         rwkv_r_k, rwkv_lnx_g, rwkv_lnx_b).astype(x.dtype) @ w_o_rwkv
    y_c = mamba2_mixer(pc, ssm_conv_w, ssm_conv_b, ssm_dt_bias, ssm_a_log, ssm_d,
                       ssm_norm_g).astype(x.dtype) @ w_o_ssm
    gates = jax.nn.sigmoid(pg).reshape(bsz, seq, N_BRANCH, D_MODEL)
    merged = gates[:, :, 0] * y_a + gates[:, :, 1] * y_b + gates[:, :, 2] * y_c
    x = x + rms_norm((merged @ w_out).astype(x.dtype), post_mix_g)
    hf = rms_norm(x, pre_ffn_g)
    f = jnp.square(jax.nn.relu(hf @ w_ff1)) @ w_ff2
    return x + rms_norm(f.astype(x.dtype), post_ffn_g)


def setup_inputs(seed: int = 0) -> dict:
    key = jax.random.key(seed)
    ks = iter(jax.random.split(key, 40))
    f32 = jnp.float32
    Ld = DEPTH

    def nrm(shape, scale):
        return jax.random.normal(next(ks), shape, f32) * scale

    def unif(shape, lo, hi):
        return jax.random.uniform(next(ks), shape, f32, lo, hi)

    def gain(shape):
        return 1.0 + nrm(shape, 0.02)

    x = nrm((BATCH, SEQ, D_MODEL), 1.0)
    pre_mix_g = gain((Ld, D_MODEL))
    w_in = nrm((Ld, D_MODEL, IN_COLS), D_MODEL ** -0.5)
    w_o_att = nrm((Ld, ATT_WIDTH, D_MODEL), ATT_WIDTH ** -0.5)
    rwkv_mu = unif((Ld, B_COLS), 0.0, 1.0)
    rwkv_w0 = unif((Ld, RWKV_WIDTH), -5.0, 0.5)
    rwkv_w2 = nrm((Ld, DECAY_LORA, RWKV_WIDTH), 0.1 * DECAY_LORA ** -0.5)
    rwkv_a0 = nrm((Ld, RWKV_WIDTH), 0.1)
    rwkv_a2 = nrm((Ld, AAA_LORA, RWKV_WIDTH), 0.1 * AAA_LORA ** -0.5)
    rwkv_g2 = nrm((Ld, GATE_LORA, RWKV_WIDTH), GATE_LORA ** -0.5)
    rwkv_k_k = 0.85 + nrm((Ld, RWKV_WIDTH), 0.02)
    rwkv_k_a = gain((Ld, RWKV_WIDTH))
    rwkv_r_k = nrm((Ld, RWKV_HEADS, RWKV_HEAD), 0.1)
    rwkv_lnx_g = gain((Ld, RWKV_WIDTH))
    rwkv_lnx_b = nrm((Ld, RWKV_WIDTH), 0.02)
    w_o_rwkv = nrm((Ld, RWKV_WIDTH, D_MODEL), RWKV_WIDTH ** -0.5)
    ssm_conv_w = nrm((Ld, SSM_CONV, SSM_CONV_CH), SSM_CONV ** -0.5)
    ssm_conv_b = nrm((Ld, SSM_CONV_CH), 0.02)
    dt0 = jnp.exp(unif((Ld, SSM_HEADS), math.log(1e-3), math.log(1e-1)))
    ssm_dt_bias = dt0 + jnp.log(-jnp.expm1(-dt0))
    ssm_a_log = jnp.log(unif((Ld, SSM_HEADS), 1.0, 16.0))
    ssm_d = gain((Ld, SSM_HEADS))
    ssm_norm_g = gain((Ld, SSM_INNER))
    w_o_ssm = nrm((Ld, SSM_INNER, D_MODEL), SSM_INNER ** -0.5)
    w_out = nrm((Ld, D_MODEL, D_MODEL), D_MODEL ** -0.5)
    post_mix_g = gain((Ld, D_MODEL))
    pre_ffn_g = gain((Ld, D_MODEL))
    w_ff1 = nrm((Ld, D_MODEL, D_FF), D_MODEL ** -0.5)
    w_ff2 = nrm((Ld, D_FF, D_MODEL), D_FF ** -0.5)
    post_ffn_g = gain((Ld, D_MODEL))
    return {"x": x, "pre_mix_g": pre_mix_g, "w_in": w_in, "w_o_att": w_o_att, "rwkv_mu": rwkv_mu,
            "rwkv_w0": rwkv_w0, "rwkv_w2": rwkv_w2, "rwkv_a0": rwkv_a0, "rwkv_a2": rwkv_a2,
            "rwkv_g2": rwkv_g2, "rwkv_k_k": rwkv_k_k, "rwkv_k_a": rwkv_k_a, "rwkv_r_k": rwkv_r_k,
            "rwkv_lnx_g": rwkv_lnx_g, "rwkv_lnx_b": rwkv_lnx_b, "w_o_rwkv": w_o_rwkv,
            "ssm_conv_w": ssm_conv_w, "ssm_conv_b": ssm_conv_b, "ssm_dt_bias": ssm_dt_bias,
            "ssm_a_log": ssm_a_log, "ssm_d": ssm_d, "ssm_norm_g": ssm_norm_g, "w_o_ssm": w_o_ssm,
            "w_out": w_out, "post_mix_g": post_mix_g, "pre_ffn_g": pre_ffn_g, "w_ff1": w_ff1,
            "w_ff2": w_ff2, "post_ffn_g": post_ffn_g}


def reference(x, pre_mix_g, w_in, w_o_att, rwkv_mu, rwkv_w0, rwkv_w2, rwkv_a0, rwkv_a2, rwkv_g2,
              rwkv_k_k, rwkv_k_a, rwkv_r_k, rwkv_lnx_g, rwkv_lnx_b, w_o_rwkv, ssm_conv_w, ssm_conv_b,
              ssm_dt_bias, ssm_a_log, ssm_d, ssm_norm_g, w_o_ssm, w_out, post_mix_g, pre_ffn_g,
              w_ff1, w_ff2, post_ffn_g):
    for i in range(DEPTH):
        x = hybrid_layer(x, pre_mix_g[i], w_in[i], w_o_att[i], rwkv_mu[i], rwkv_w0[i], rwkv_w2[i],
                         rwkv_a0[i], rwkv_a2[i], rwkv_g2[i], rwkv_k_k[i], rwkv_k_a[i], rwkv_r_k[i],
                         rwkv_lnx_g[i], rwkv_lnx_b[i], w_o_rwkv[i], ssm_conv_w[i], ssm_conv_b[i],
                         ssm_dt_bias[i], ssm_a_log[i], ssm_d[i], ssm_norm_g[i], w_o_ssm[i], w_out[i],
                         post_mix_g[i], pre_ffn_g[i], w_ff1[i], w_ff2[i], post_ffn_g[i])
    return x
```

```python
import functools
import math

import jax
import jax.numpy as jnp
import numpy as np
from jax import lax
from jax.experimental import pallas as pl
from jax.experimental.pallas import tpu as pltpu

F32 = jnp.float32
BF16 = jnp.bfloat16

D_MODEL = 1024
ATT_HEADS = 8
ATT_HEAD_DIM = 128
ATT_WIDTH = ATT_HEADS * ATT_HEAD_DIM
IDX_HEADS = 4
IDX_HEAD_DIM = 64
TOPK_MAX = 256
ROPE_THETA = 500000.0
ATT_ROT = ATT_HEAD_DIM // 4
IDX_ROT = IDX_HEAD_DIM // 4
RWKV_HEAD = 64
RWKV_WIDTH = D_MODEL
RWKV_HEADS = RWKV_WIDTH // RWKV_HEAD
DECAY_LORA = 64
AAA_LORA = 64
GATE_LORA = 128
RWKV_GN_EPS = 64e-5
SSM_INNER = 2 * D_MODEL
SSM_HEAD_DIM = 64
SSM_HEADS = SSM_INNER // SSM_HEAD_DIM
SSM_GROUPS = 2
SSM_STATE = 128
SSM_CONV = 4
SSM_CHUNK = 128
SSM_CONV_CH = SSM_INNER + 2 * SSM_GROUPS * SSM_STATE
D_FF = 4 * D_MODEL
N_BRANCH = 3
NORM_EPS = 1e-6

V7X_VMEM_BYTES = 64 * 1024 * 1024
VMEM_LIMIT = 56 * 1024 * 1024

INT32_MIN = np.int32(-(2 ** 31))
MASKED_LOGIT = -1e30


def _cparams(sem):
    return pltpu.CompilerParams(dimension_semantics=sem, vmem_limit_bytes=VMEM_LIMIT)


def _split2(a):
    hi = a.astype(BF16)
    lo = (a - hi.astype(F32)).astype(BF16)
    return hi, lo


_NN = (((1,), (0,)), ((), ()))
_NT = (((1,), (1,)), ((), ()))
_TN = (((0,), (0,)), ((), ()))


def _dot(a, b, dims=_NN):
    return lax.dot_general(a, b, dims, preferred_element_type=F32)


def _dot_b(a, b, dims=_NN):
    return _dot(a.astype(BF16), b.astype(BF16), dims)


def _dot3(a, b, dims=_NN):
    ah, al = _split2(a)
    bh, bl = _split2(b)
    return _dot(ah, bh, dims) + (_dot(ah, bl, dims) + _dot(al, bh, dims))


def _rmsnorm_kernel(x_ref, g_ref, o_ref):
    x = x_ref[...]
    y = x * lax.rsqrt(jnp.mean(x * x, axis=-1, keepdims=True) + NORM_EPS)
    o_ref[...] = (y * g_ref[...]).astype(o_ref.dtype)


def _rmsnorm(x, g, out_dtype=BF16, tm=1024):
    n, d = x.shape
    return pl.pallas_call(
        _rmsnorm_kernel,
        grid=(n // tm,),
        in_specs=[pl.BlockSpec((tm, d), lambda i: (i, 0)), pl.BlockSpec((1, d), lambda i: (0, 0))],
        out_specs=pl.BlockSpec((tm, d), lambda i: (i, 0)),
        out_shape=jax.ShapeDtypeStruct((n, d), out_dtype),
        compiler_params=_cparams(("parallel",)),
        name="rmsnorm",
    )(x, g.reshape(1, d))


def _mm_kernel(x_ref, w_ref, o_ref, *, act):
    y = _dot(x_ref[...], w_ref[...])
    if act == "relu2":
        y = jnp.square(jnp.maximum(y, 0.0))
    o_ref[...] = y.astype(o_ref.dtype)


def _pick_tile(n, prefs):
    for t in prefs:
        if n % t == 0:
            return t
    return n


def _mm(x, w, out_dtype=F32, act=None):
    m, k = x.shape
    _, n = w.shape
    x = x.astype(BF16)
    w = w.astype(BF16)
    tm = _pick_tile(m, (1024, 512, 256, 128))
    tn = _pick_tile(n, (1024, 768, 512, 256, 128))
    return pl.pallas_call(
        functools.partial(_mm_kernel, act=act),
        grid=(n // tn, m // tm),
        in_specs=[pl.BlockSpec((tm, k), lambda j, i: (i, 0)), pl.BlockSpec((k, tn), lambda j, i: (0, j))],
        out_specs=pl.BlockSpec((tm, tn), lambda j, i: (i, j)),
        out_shape=jax.ShapeDtypeStruct((m, n), out_dtype),
        compiler_params=_cparams(("parallel", "parallel")),
        name="matmul",
    )(x, w)


def _mm_norm_res_kernel(x_ref, w_ref, g_ref, r_ref, o_ref):
    y = _dot(x_ref[...], w_ref[...])
    y = y * lax.rsqrt(jnp.mean(y * y, axis=-1, keepdims=True) + NORM_EPS)
    o_ref[...] = r_ref[...] + y * g_ref[...]


def _mm_norm_res(x, w, g, res, tm=512):
    m, k = x.shape
    _, n = w.shape
    x = x.astype(BF16)
    w = w.astype(BF16)
    return pl.pallas_call(
        _mm_norm_res_kernel,
        grid=(m // tm,),
        in_specs=[pl.BlockSpec((tm, k), lambda i: (i, 0)), pl.BlockSpec((k, n), lambda i: (0, 0)),
                  pl.BlockSpec((1, n), lambda i: (0, 0)), pl.BlockSpec((tm, n), lambda i: (i, 0))],
        out_specs=pl.BlockSpec((tm, n), lambda i: (i, 0)),
        out_shape=jax.ShapeDtypeStruct((m, n), F32),
        compiler_params=_cparams(("parallel",)),
        name="matmul_norm_residual",
    )(x, w, g.reshape(1, n), res)


DSA_TQ = 128
DSA_CK = 512


def _dsa_mask_kernel(qi_ref, wi_ref, kit_ref, tri_ref, mask_ref, key_ref, *, seq, topk):
    tq, ck = DSA_TQ, DSA_CK
    qb = pl.program_id(1)
    nch = qb // (ck // tq) + 1
    qpos = qb * tq + lax.broadcasted_iota(jnp.int32, (tq, ck), 0)
    kiota = lax.broadcasted_iota(jnp.int32, (tq, ck), 1)
    idx_scale = IDX_HEAD_DIM ** -0.5
    qi = qi_ref[0]
    wi = wi_ref[0]
    q_parts = [_split2(qi[:, h * IDX_HEAD_DIM:(h + 1) * IDX_HEAD_DIM]) for h in range(IDX_HEADS)]

    def score_chunk(c, carry):
        off = pl.multiple_of(c * ck, ck)
        kt = kit_ref[0, :, pl.ds(off, ck)]
        kh, kl = _split2(kt)
        s = jnp.zeros((tq, ck), F32)
        for h in range(IDX_HEADS):
            qh, ql = q_parts[h]
            d = _dot(qh, kh) + (_dot(qh, kl) + _dot(ql, kh))
            s = s + jnp.maximum(d * idx_scale, 0.0) * wi[:, h:h + 1]
        s = s + 0.0
        bits = pltpu.bitcast(s, jnp.int32)
        key = bits ^ ((bits >> 31) & jnp.int32(0x7FFFFFFF))
        key_ref[:, pl.ds(off, ck)] = jnp.where(off + kiota <= qpos, key, INT32_MIN)
        return carry

    lax.fori_loop(0, nch, score_chunk, 0)

    def count(pred):
        def body(c, acc):
            off = pl.multiple_of(c * ck, ck)
            m = jnp.where(pred(key_ref[:, pl.ds(off, ck)]), 1, 0)
            for j in range(ck // 128):
                acc = acc + m[:, j * 128:(j + 1) * 128]
            return acc
        acc = lax.fori_loop(0, nch, body, jnp.zeros((tq, 128), jnp.int32))
        return jnp.sum(acc, axis=1, keepdims=True)

    def bit_body(i, lo):
        cand = lo + (jnp.int32(1) << (31 - i))
        cnt = count(lambda k: k >= cand)
        return jnp.where(cnt >= topk, cand, lo)

    tau = lax.fori_loop(0, 32, bit_body, jnp.full((tq, 1), INT32_MIN, jnp.int32))
    need = (topk - count(lambda k: k > tau)).astype(F32)

    def mask_chunk(c, run):
        off = pl.multiple_of(c * ck, ck)
        k = key_ref[:, pl.ds(off, ck)]
        causal = off + kiota <= qpos
        eq = jnp.logical_and(k == tau, causal)
        eqf = jnp.where(eq, 1.0, 0.0)
        before = _dot(eqf.astype(BF16), tri_ref[...]) + run
        take = jnp.logical_or(k > tau, jnp.logical_and(eq, before < need))
        mask_ref[0, :, pl.ds(off, ck)] = jnp.where(take, 1, 0).astype(jnp.int8)
        return run + jnp.sum(eqf, axis=1, keepdims=True)

    lax.fori_loop(0, nch, mask_chunk, jnp.zeros((tq, 1), F32))

    def zero_chunk(c, carry):
        off = pl.multiple_of(c * ck, ck)
        mask_ref[0, :, pl.ds(off, ck)] = jnp.zeros((tq, ck), jnp.int8)
        return carry

    lax.fori_loop(nch, seq // ck, zero_chunk, 0)


def _dsa_mask(qi, wi, kit, topk):
    bsz, seq, _ = qi.shape
    tq, ck = DSA_TQ, DSA_CK
    tri = jnp.asarray(np.triu(np.ones((ck, ck), np.float32), 1), BF16)
    return pl.pallas_call(
        functools.partial(_dsa_mask_kernel, seq=seq, topk=topk),
        grid=(bsz, seq // tq),
        in_specs=[pl.BlockSpec((1, tq, IDX_HEADS * IDX_HEAD_DIM), lambda b, q: (b, q, 0)),
                  pl.BlockSpec((1, tq, IDX_HEADS), lambda b, q: (b, q, 0)),
                  pl.BlockSpec((1, IDX_HEAD_DIM, seq), lambda b, q: (b, 0, 0)),
                  pl.BlockSpec((ck, ck), lambda b, q: (0, 0))],
        out_specs=pl.BlockSpec((1, tq, seq), lambda b, q: (b, q, 0)),
        out_shape=jax.ShapeDtypeStruct((bsz, seq, seq), jnp.int8),
        scratch_shapes=[pltpu.VMEM((tq, seq), jnp.int32)],
        compiler_params=_cparams(("parallel", "parallel")),
        name="dsa_topk_mask",
    )(qi, wi, kit, tri)


ATT_TQ = 256
ATT_TK = 512


def _dsa_attn_kernel(q_ref, k_ref, v_ref, m_ref, o_ref, m_sc, l_sc, acc_sc):
    tq, tk = ATT_TQ, ATT_TK
    qb = pl.program_id(1)
    kb = pl.program_id(2)

    @pl.when(kb == 0)
    def _():
        m_sc[...] = jnp.full(m_sc.shape, MASKED_LOGIT, F32)
        l_sc[...] = jnp.zeros(l_sc.shape, F32)
        acc_sc[...] = jnp.zeros(acc_sc.shape, F32)

    @pl.when(kb * tk <= qb * tq + (tq - 1))
    def _():
        sel = m_ref[0].astype(jnp.int32) != 0
        for h in range(ATT_HEADS):
            cs = slice(h * ATT_HEAD_DIM, (h + 1) * ATT_HEAD_DIM)
            s = _dot(q_ref[0, :, cs], k_ref[0, :, cs], _NT)
            s = jnp.where(sel, s, MASKED_LOGIT)
            m_prev = m_sc[h]
            m_new = jnp.maximum(m_prev, jnp.max(s, axis=1, keepdims=True))
            p = jnp.where(sel, jnp.exp(s - m_new), 0.0)
            alpha = jnp.exp(m_prev - m_new)
            l_sc[h] = alpha * l_sc[h] + jnp.sum(p, axis=1, keepdims=True)
            acc_sc[:, cs] = alpha * acc_sc[:, cs] + _dot(p.astype(BF16), v_ref[0, :, cs])
            m_sc[h] = m_new

    @pl.when(kb == pl.num_programs(2) - 1)
    def _():
        for h in range(ATT_HEADS):
            cs = slice(h * ATT_HEAD_DIM, (h + 1) * ATT_HEAD_DIM)
            o_ref[0, :, cs] = (acc_sc[:, cs] / l_sc[h]).astype(o_ref.dtype)


def _dsa_attend(q, k, v, mask, out_dtype=BF16):
    bsz, seq, width = q.shape
    tq, tk = ATT_TQ, ATT_TK

    def kv_idx(b, i, j):
        return (b, jnp.minimum(j, (i * tq + tq - 1) // tk), 0)

    def m_idx(b, i, j):
        return (b, i, jnp.minimum(j, (i * tq + tq - 1) // tk))

    return pl.pallas_call(
        _dsa_attn_kernel,
        grid=(bsz, seq // tq, seq // tk),
        in_specs=[pl.BlockSpec((1, tq, width), lambda b, i, j: (b, i, 0)),
                  pl.BlockSpec((1, tk, width), kv_idx),
                  pl.BlockSpec((1, tk, width), kv_idx),
                  pl.BlockSpec((1, tq, tk), m_idx)],
        out_specs=pl.BlockSpec((1, tq, width), lambda b, i, j: (b, i, 0)),
        out_shape=jax.ShapeDtypeStruct((bsz, seq, width), out_dtype),
        scratch_shapes=[pltpu.VMEM((ATT_HEADS, tq, 1), F32), pltpu.VMEM((ATT_HEADS, tq, 1), F32),
                        pltpu.VMEM((tq, width), F32)],
        compiler_params=_cparams(("parallel", "parallel", "arbitrary")),
        name="dsa_attention",
    )(q, k, v, mask)


def _rope_partial(x, pos, rot):
    half = rot // 2
    inv = ROPE_THETA ** (-jnp.arange(half, dtype=F32) * 2.0 / rot)
    ang = pos.astype(F32)[:, None] * inv[None, :]
    cos = jnp.cos(ang)[:, None, :]
    sin = jnp.sin(ang)[:, None, :]
    xr = x[..., :rot]
    x1, x2 = xr[..., :half], xr[..., half:]
    rotated = jnp.concatenate([x1 * cos - x2 * sin, x2 * cos + x1 * sin], axis=-1)
    return jnp.concatenate([rotated, x[..., rot:]], axis=-1)


def _dsa_branch(q, k, v, qi, ki, wi):
    bsz, seq, _ = q.shape
    pos = jnp.arange(seq)
    topk = min(TOPK_MAX, seq // 4)
    q = _rope_partial(q.reshape(bsz, seq, ATT_HEADS, ATT_HEAD_DIM), pos, ATT_ROT).reshape(bsz, seq, ATT_WIDTH)
    k = _rope_partial(k.reshape(bsz, seq, ATT_HEADS, ATT_HEAD_DIM), pos, ATT_ROT).reshape(bsz, seq, ATT_WIDTH)
    qi = _rope_partial(qi.reshape(bsz, seq, IDX_HEADS, IDX_HEAD_DIM), pos, IDX_ROT)
    qi = qi.reshape(bsz, seq, IDX_HEADS * IDX_HEAD_DIM)
    ki = _rope_partial(ki.reshape(bsz, seq, 1, IDX_HEAD_DIM), pos, IDX_ROT)[:, :, 0]
    wi = wi * (IDX_HEADS ** -0.5)
    mask = _dsa_mask(qi, wi, ki.swapaxes(1, 2), topk)
    qs = (q * (ATT_HEAD_DIM ** -0.5)).astype(BF16)
    return _dsa_attend(qs, k.astype(BF16), v.astype(BF16), mask)


RW_L = 64


def _rwkv_kernel(r_ref, lw_ref, k_ref, v_ref, a_ref, b_ref, tril_ref, y_ref, s_ref):
    L, hd = RW_L, RWKV_HEAD

    @pl.when(pl.program_id(1) == 0)
    def _():
        s_ref[...] = jnp.zeros(s_ref.shape, F32)

    lw = lw_ref[0]
    tril = tril_ref[...]
    h1 = lw.astype(BF16)
    r1 = lw - h1.astype(F32)
    h2 = r1.astype(BF16)
    h3 = (r1 - h2.astype(F32)).astype(BF16)
    cum = _dot(tril, h1) + (_dot(tril, h2) + _dot(tril, h3))
    tot = cum[L - 1:L, :]
    e_neg = jnp.exp(-cum)
    e_end = jnp.exp(tot - cum)
    w_tot = jnp.exp(tot)
    rt = r_ref[0] * jnp.exp(cum)
    at = a_ref[0] * jnp.exp(cum - lw)
    bt = b_ref[0] * e_neg
    kt = k_ref[0] * e_neg
    bh = b_ref[0] * e_end
    kh = k_ref[0] * e_end
    v = v_ref[0]

    row = lax.broadcasted_iota(jnp.int32, (2 * L, L), 0)
    col = lax.broadcasted_iota(jnp.int32, (2 * L, L), 1)
    keep = col < jnp.where(row < L, row, row - (L - 1))
    eye = lax.broadcasted_iota(jnp.int32, (L, L), 0) == lax.broadcasted_iota(jnp.int32, (L, L), 1)

    for h in range(RWKV_HEADS):
        cs = slice(h * hd, (h + 1) * hd)
        lhs = jnp.concatenate([at[:, cs], rt[:, cs]], axis=0)
        a_b = jnp.where(keep, _dot3(lhs, bt[:, cs], _NT), 0.0)
        a_k = jnp.where(keep, _dot3(lhs, kt[:, cs], _NT), 0.0)
        n = a_b[:L]
        t_inv = jnp.where(eye, 1.0, 0.0) + n
        for _ in range(int(math.log2(L)) - 1):
            n = _dot3(n, n)
            t_inv = t_inv + _dot3(t_inv, n)
        vh = v[:, cs]
        akv = _dot3(a_k, vh)
        p12 = _dot3(t_inv, jnp.concatenate([at[:, cs], akv[:L]], axis=1))
        q12 = _dot3(a_b[L:], p12) + jnp.concatenate([rt[:, cs], akv[L:]], axis=1)
        s = s_ref[h]
        y_ref[0, :, cs] = _dot3(q12[:, :hd], s, _NT) + q12[:, hd:]
        g = _dot3(p12, bh[:, cs], _TN)
        s_ref[h] = s * w_tot[:, cs] + _dot3(s, g[:hd]) + g[hd:] + _dot3(vh, kh[:, cs], _TN)


def _rwkv_scan(r, lw, k, v, a_vec, b_vec):
    bsz, seq, width = r.shape
    L = RW_L
    tril = jnp.asarray(np.tril(np.ones((L, L), np.float32)), BF16)
    spec = pl.BlockSpec((1, L, width), lambda b, c: (b, c, 0))
    return pl.pallas_call(
        _rwkv_kernel,
        grid=(bsz, seq // L),
        in_specs=[spec] * 6 + [pl.BlockSpec((L, L), lambda b, c: (0, 0))],
        out_specs=spec,
        out_shape=jax.ShapeDtypeStruct((bsz, seq, width), F32),
        scratch_shapes=[pltpu.VMEM((RWKV_HEADS, RWKV_HEAD, RWKV_HEAD), F32)],
        compiler_params=_cparams(("parallel", "arbitrary")),
        name="rwkv7_scan",
    )(r, lw, k, v, a_vec, b_vec, tril)


def _rwkv_branch(p, mu, w0, w2, a0, a2, g2, k_k, k_a, r_k, lnx_g, lnx_b):
    bsz, seq, _ = p.shape
    n = bsz * seq
    prev = jnp.pad(p, ((0, 0), (1, 0), (0, 0)))[:, :-1]
    p = p + (prev - p) * mu
    r, k, v, wl, al, gl = jnp.split(p, np.cumsum([RWKV_WIDTH] * 3 + [DECAY_LORA, AAA_LORA]).tolist(), axis=-1)
    w = -jax.nn.softplus(-(w0 + _mm(jnp.tanh(wl).reshape(n, -1), w2).reshape(bsz, seq, -1))) - 0.5
    lw = -jnp.exp(w)
    a = jax.nn.sigmoid(a0 + _mm(al.reshape(n, -1), a2).reshape(bsz, seq, -1))
    g = _mm(jax.nn.sigmoid(gl).reshape(n, -1), g2).reshape(bsz, seq, -1)

    def heads(t):
        return t.reshape(bsz, seq, RWKV_HEADS, RWKV_HEAD)

    kk = heads(k * k_k)
    kk = kk / jnp.maximum(jnp.sqrt(jnp.sum(kk * kk, axis=-1, keepdims=True)), 1e-12)
    kk = kk.reshape(bsz, seq, RWKV_WIDTH)
    k = k * (1.0 + (a - 1.0) * k_a)
    y = heads(_rwkv_scan(r, lw, k, v, -kk, kk * a))
    mean = jnp.mean(y, axis=-1, keepdims=True)
    var = jnp.mean(jnp.square(y - mean), axis=-1, keepdims=True)
    y = ((y - mean) * lax.rsqrt(var + RWKV_GN_EPS)).reshape(bsz, seq, RWKV_WIDTH) * lnx_g + lnx_b
    bonus = (jnp.sum(heads(r) * heads(k) * r_k, axis=-1, keepdims=True) * heads(v)).reshape(bsz, seq, RWKV_WIDTH)
    return (y + bonus) * g


def _ssd_kernel(xdt_ref, ae_ref, at_ref, b_ref, c_ref, tril_ref, y_ref, h_ref):
    L, hd, ns = SSM_CHUNK, SSM_HEAD_DIM, SSM_STATE
    hg = SSM_HEADS // SSM_GROUPS
    gw = hg * hd

    @pl.when(pl.program_id(1) == 0)
    def _():
        h_ref[...] = jnp.zeros(h_ref.shape, F32)

    def cumsum3(mat, x, dims):
        h1 = x.astype(BF16)
        r1 = x - h1.astype(F32)
        h2 = r1.astype(BF16)
        h3 = (r1 - h2.astype(F32)).astype(BF16)
        if dims is _NN:
            return _dot(mat, h1) + (_dot(mat, h2) + _dot(mat, h3))
        return _dot(h1, mat, dims) + (_dot(h2, mat, dims) + _dot(h3, mat, dims))

    tril = tril_ref[...]
    a_cum = cumsum3(tril, ae_ref[0], _NN)
    a_cum_t = cumsum3(tril, at_ref[0], _NT)
    tot = a_cum[L - 1:L, :]
    e_in = jnp.exp(a_cum)
    xw = xdt_ref[0] * jnp.exp(tot - a_cum)
    e_tot = jnp.exp(tot)
    lower = lax.broadcasted_iota(jnp.int32, (L, L), 1) <= lax.broadcasted_iota(jnp.int32, (L, L), 0)

    for g in range(SSM_GROUPS):
        gs = slice(g * gw, (g + 1) * gw)
        bg = b_ref[0, :, g * ns:(g + 1) * ns]
        cg = c_ref[0, :, g * ns:(g + 1) * ns]
        cb = _dot_b(cg, bg, _NT)
        ht = h_ref[g]
        y_off = _dot_b(cg, ht) * e_in[:, gs]
        for j in range(hg):
            h = g * hg + j
            cs = slice(h * hd, (h + 1) * hd)
            seg = a_cum[:, h * hd:h * hd + 1] - a_cum_t[h:h + 1, :]
            m = cb * jnp.exp(jnp.where(lower, seg, -jnp.inf))
            y_ref[0, :, cs] = _dot_b(m, xdt_ref[0, :, cs]) + y_off[:, j * hd:(j + 1) * hd]
        h_ref[g] = ht * e_tot[:, gs] + _dot_b(bg, xw[:, gs], _TN)


def _ssd_scan(xdt, a_exp, a_t, bm, cm):
    bsz, seq, width = xdt.shape
    L = SSM_CHUNK
    gn = SSM_GROUPS * SSM_STATE
    tril = jnp.asarray(np.tril(np.ones((L, L), np.float32)), BF16)
    wide = pl.BlockSpec((1, L, width), lambda b, c: (b, c, 0))
    bc = pl.BlockSpec((1, L, gn), lambda b, c: (b, c, 0))
    return pl.pallas_call(
        _ssd_kernel,
        grid=(bsz, seq // L),
        in_specs=[wide, wide, pl.BlockSpec((1, SSM_HEADS, L), lambda b, c: (b, 0, c)), bc, bc,
                  pl.BlockSpec((L, L), lambda b, c: (0, 0))],
        out_specs=wide,
        out_shape=jax.ShapeDtypeStruct((bsz, seq, width), F32),
        scratch_shapes=[pltpu.VMEM((SSM_GROUPS, SSM_STATE, width // SSM_GROUPS), F32)],
        compiler_params=_cparams(("parallel", "arbitrary")),
        name="mamba2_ssd",
    )(xdt, a_exp, a_t, bm, cm, tril)


def _mamba_branch(p, conv_w, conv_b, dt_bias, a_log, d_skip, norm_g):
    bsz, seq, _ = p.shape
    z, xbc, dt = jnp.split(p, [SSM_INNER, SSM_INNER + SSM_CONV_CH], axis=-1)
    xpad = jnp.pad(xbc, ((0, 0), (SSM_CONV - 1, 0), (0, 0)))
    xbc = sum(xpad[:, j:j + seq] * conv_w[j] for j in range(SSM_CONV)) + conv_b
    xbc = jax.nn.silu(xbc)
    xs, bm, cm = jnp.split(xbc, [SSM_INNER, SSM_INNER + SSM_GROUPS * SSM_STATE], axis=-1)
    dt = jax.nn.softplus(dt + dt_bias)
    a = dt * (-jnp.exp(a_log))
    xh = xs.reshape(bsz, seq, SSM_HEADS, SSM_HEAD_DIM)
    xdt = (xh * dt[..., None]).reshape(bsz, seq, SSM_INNER)
    a_exp = jnp.repeat(a, SSM_HEAD_DIM, axis=-1)
    y = _ssd_scan(xdt, a_exp, a.swapaxes(1, 2), bm, cm).reshape(bsz, seq, SSM_HEADS, SSM_HEAD_DIM)
    y = y + xh * d_skip[:, None]
    y = y.reshape(bsz, seq, SSM_INNER) * jax.nn.silu(z)
    yg = y.reshape(bsz, seq, SSM_GROUPS, SSM_INNER // SSM_GROUPS)
    yg = yg * lax.rsqrt(jnp.mean(yg * yg, axis=-1, keepdims=True) + NORM_EPS)
    return yg.reshape(bsz, seq, SSM_INNER) * norm_g


A_SIZES = [ATT_WIDTH] * 3 + [IDX_HEADS * IDX_HEAD_DIM, IDX_HEAD_DIM, IDX_HEADS]
B_COLS = 3 * RWKV_WIDTH + DECAY_LORA + AAA_LORA + GATE_LORA
C_COLS = SSM_INNER + SSM_CONV_CH + SSM_HEADS
GATE_COLS = N_BRANCH * D_MODEL
LANE = 128


def _pad_cols(w, mult=LANE):
    pad = (-w.shape[1]) % mult
    return jnp.pad(w, ((0, 0), (0, pad))) if pad else w


def _layer(x, bsz, seq, pre_mix_g, w_in, w_o_att, rwkv_mu, rwkv_w0, rwkv_w2, rwkv_a0, rwkv_a2, rwkv_g2,
           rwkv_k_k, rwkv_k_a, rwkv_r_k, rwkv_lnx_g, rwkv_lnx_b, w_o_rwkv, ssm_conv_w, ssm_conv_b,
           ssm_dt_bias, ssm_a_log, ssm_d, ssm_norm_g, w_o_ssm, w_out, post_mix_g, pre_ffn_g,
           w_ff1, w_ff2, post_ffn_g):
    n = bsz * seq
    a_cols = sum(A_SIZES)
    offs = np.cumsum([a_cols, B_COLS, C_COLS]).tolist()
    wa, wb, wc, wg = jnp.split(w_in, offs, axis=1)
    parts = [_pad_cols(wa), _pad_cols(wb), _pad_cols(wc), wg]
    widths = [p.shape[1] for p in parts]
    h = _rmsnorm(x, pre_mix_g)
    proj = _mm(h, jnp.concatenate(parts, axis=1))
    o = np.cumsum([0] + widths).tolist()
    pa = proj[:, o[0]:o[0] + a_cols].reshape(bsz, seq, a_cols)
    pb = proj[:, o[1]:o[1] + B_COLS].reshape(bsz, seq, B_COLS)
    pc = proj[:, o[2]:o[2] + C_COLS].reshape(bsz, seq, C_COLS)
    pg = proj[:, o[3]:o[3] + GATE_COLS]

    q, k, v, qi, ki, wi = jnp.split(pa, np.cumsum(A_SIZES)[:-1].tolist(), axis=-1)
    y_a = _mm(_dsa_branch(q, k, v, qi, ki, wi).reshape(n, ATT_WIDTH), w_o_att)
    y_b = _mm(_rwkv_branch(pb, rwkv_mu, rwkv_w0, rwkv_w2, rwkv_a0, rwkv_a2, rwkv_g2, rwkv_k_k, rwkv_k_a,
                           rwkv_r_k, rwkv_lnx_g, rwkv_lnx_b).reshape(n, RWKV_WIDTH), w_o_rwkv)
    y_c = _mm(_mamba_branch(pc, ssm_conv_w, ssm_conv_b, ssm_dt_bias, ssm_a_log, ssm_d,
                            ssm_norm_g).reshape(n, SSM_INNER), w_o_ssm)
    gates = jax.nn.sigmoid(pg)
    merged = (gates[:, :D_MODEL] * y_a + gates[:, D_MODEL:2 * D_MODEL] * y_b + gates[:, 2 * D_MODEL:] * y_c)
    x = _mm_norm_res(merged, w_out, post_mix_g, x)
    hf = _rmsnorm(x, pre_ffn_g)
    f = _mm(hf, w_ff1, out_dtype=BF16, act="relu2")
    return _mm_norm_res(f, w_ff2, post_ffn_g, x)


def kernel(x, pre_mix_g, w_in, w_o_att, rwkv_mu, rwkv_w0, rwkv_w2, rwkv_a0, rwkv_a2, rwkv_g2, rwkv_k_k,
           rwkv_k_a, rwkv_r_k, rwkv_lnx_g, rwkv_lnx_b, w_o_rwkv, ssm_conv_w, ssm_conv_b, ssm_dt_bias,
           ssm_a_log, ssm_d, ssm_norm_g, w_o_ssm, w_out, post_mix_g, pre_ffn_g, w_ff1, w_ff2, post_ffn_g):
    bsz, seq, d = x.shape
    params = (pre_mix_g, w_in, w_o_att, rwkv_mu, rwkv_w0, rwkv_w2, rwkv_a0, rwkv_a2, rwkv_g2, rwkv_k_k,
              rwkv_k_a, rwkv_r_k, rwkv_lnx_g, rwkv_lnx_b, w_o_rwkv, ssm_conv_w, ssm_conv_b, ssm_dt_bias,
              ssm_a_log, ssm_d, ssm_norm_g, w_o_ssm, w_out, post_mix_g, pre_ffn_g, w_ff1, w_ff2, post_ffn_g)
    y = x.reshape(bsz * seq, d)
    for i in range(pre_mix_g.shape[0]):
        y = _layer(y, bsz, seq, *[p[i] for p in params])
    return y.reshape(bsz, seq, d)
```

```python
import functools
import math

import jax
import jax.numpy as jnp
import numpy as np
from jax import lax
from jax.experimental import pallas as pl
from jax.experimental.pallas import tpu as pltpu

F32 = jnp.float32
BF16 = jnp.bfloat16

D_MODEL = 1024
ATT_HEADS = 8
ATT_HEAD_DIM = 128
ATT_WIDTH = ATT_HEADS * ATT_HEAD_DIM
IDX_HEADS = 4
IDX_HEAD_DIM = 64
TOPK_MAX = 256
ROPE_THETA = 500000.0
ATT_ROT = ATT_HEAD_DIM // 4
IDX_ROT = IDX_HEAD_DIM // 4
RWKV_HEAD = 64
RWKV_WIDTH = D_MODEL
RWKV_HEADS = RWKV_WIDTH // RWKV_HEAD
DECAY_LORA = 64
AAA_LORA = 64
GATE_LORA = 128
RWKV_GN_EPS = 64e-5
SSM_INNER = 2 * D_MODEL
SSM_HEAD_DIM = 64
SSM_HEADS = SSM_INNER // SSM_HEAD_DIM
SSM_GROUPS = 2
SSM_STATE = 128
SSM_CONV = 4
SSM_CHUNK = 128
SSM_CONV_CH = SSM_INNER + 2 * SSM_GROUPS * SSM_STATE
D_FF = 4 * D_MODEL
N_BRANCH = 3
NORM_EPS = 1e-6

V7X_VMEM_BYTES = 64 * 1024 * 1024
VMEM_LIMIT = 56 * 1024 * 1024

INT32_MIN = np.int32(-(2 ** 31))
MASKED_LOGIT = -1e30


def _cparams(sem):
    return pltpu.CompilerParams(dimension_semantics=sem, vmem_limit_bytes=VMEM_LIMIT)


def _split2(a):
    hi = a.astype(BF16)
    lo = (a - hi.astype(F32)).astype(BF16)
    return hi, lo


_NN = (((1,), (0,)), ((), ()))
_NT = (((1,), (1,)), ((), ()))
_TN = (((0,), (0,)), ((), ()))


def _dot(a, b, dims=_NN):
    return lax.dot_general(a, b, dims, preferred_element_type=F32)


def _dot_b(a, b, dims=_NN):
    return _dot(a.astype(BF16), b.astype(BF16), dims)


def _dot3(a, b, dims=_NN):
    ah, al = _split2(a)
    bh, bl = _split2(b)
    return _dot(ah, bh, dims) + (_dot(ah, bl, dims) + _dot(al, bh, dims))


def _rmsnorm_kernel(x_ref, g_ref, o_ref):
    x = x_ref[...]
    y = x * lax.rsqrt(jnp.mean(x * x, axis=-1, keepdims=True) + NORM_EPS)
    o_ref[...] = (y * g_ref[...]).astype(o_ref.dtype)


def _rmsnorm(x, g, out_dtype=BF16, tm=1024):
    n, d = x.shape
    return pl.pallas_call(
        _rmsnorm_kernel,
        grid=(n // tm,),
        in_specs=[pl.BlockSpec((tm, d), lambda i: (i, 0)), pl.BlockSpec((1, d), lambda i: (0, 0))],
        out_specs=pl.BlockSpec((tm, d), lambda i: (i, 0)),
        out_shape=jax.ShapeDtypeStruct((n, d), out_dtype),
        compiler_params=_cparams(("parallel",)),
        name="rmsnorm",
    )(x, g.reshape(1, d))


def _mm_kernel(x_ref, w_ref, o_ref, *, act):
    y = _dot(x_ref[...], w_ref[...])
    if act == "relu2":
        y = jnp.square(jnp.maximum(y, 0.0))
    o_ref[...] = y.astype(o_ref.dtype)


def _pick_tile(n, prefs):
    for t in prefs:
        if n % t == 0:
            return t
    return n


def _mm(x, w, out_dtype=F32, act=None):
    m, k = x.shape
    _, n = w.shape
    x = x.astype(BF16)
    w = w.astype(BF16)
    tm = _pick_tile(m, (1024, 512, 256, 128))
    tn = _pick_tile(n, (1024, 768, 512, 256, 128))
    return pl.pallas_call(
        functools.partial(_mm_kernel, act=act),
        grid=(n // tn, m // tm),
        in_specs=[pl.BlockSpec((tm, k), lambda j, i: (i, 0)), pl.BlockSpec((k, tn), lambda j, i: (0, j))],
        out_specs=pl.BlockSpec((tm, tn), lambda j, i: (i, j)),
        out_shape=jax.ShapeDtypeStruct((m, n), out_dtype),
        compiler_params=_cparams(("parallel", "parallel")),
        name="matmul",
    )(x, w)


def _mm_norm_res_kernel(x_ref, w_ref, g_ref, r_ref, o_ref):
    y = _dot(x_ref[...], w_ref[...])
    y = y * lax.rsqrt(jnp.mean(y * y, axis=-1, keepdims=True) + NORM_EPS)
    o_ref[...] = r_ref[...] + y * g_ref[...]


def _mm_norm_res(x, w, g, res, tm=512):
    m, k = x.shape
    _, n = w.shape
    x = x.astype(BF16)
    w = w.astype(BF16)
    return pl.pallas_call(
        _mm_norm_res_kernel,
        grid=(m // tm,),
        in_specs=[pl.BlockSpec((tm, k), lambda i: (i, 0)), pl.BlockSpec((k, n), lambda i: (0, 0)),
                  pl.BlockSpec((1, n), lambda i: (0, 0)), pl.BlockSpec((tm, n), lambda i: (i, 0))],
        out_specs=pl.BlockSpec((tm, n), lambda i: (i, 0)),
        out_shape=jax.ShapeDtypeStruct((m, n), F32),
        compiler_params=_cparams(("parallel",)),
        name="matmul_norm_residual",
    )(x, w, g.reshape(1, n), res)


DSA_TQ = 128
DSA_CK = 512


def _dsa_mask_kernel(qi_ref, wi_ref, kit_ref, tri_ref, mask_ref, key_ref, *, seq, topk):
    tq, ck = DSA_TQ, DSA_CK
    qb = pl.program_id(1)
    nch = qb // (ck // tq) + 1
    qpos = qb * tq + lax.broadcasted_iota(jnp.int32, (tq, ck), 0)
    kiota = lax.broadcasted_iota(jnp.int32, (tq, ck), 1)
    idx_scale = IDX_HEAD_DIM ** -0.5
    qi = qi_ref[0]
    wi = wi_ref[0]
    q_parts = [_split2(qi[:, h * IDX_HEAD_DIM:(h + 1) * IDX_HEAD_DIM]) for h in range(IDX_HEADS)]

    def score_chunk(c, carry):
        off = pl.multiple_of(c * ck, ck)
        kt = kit_ref[0, :, pl.ds(off, ck)]
        kh, kl = _split2(kt)
        s = jnp.zeros((tq, ck), F32)
        for h in range(IDX_HEADS):
            qh, ql = q_parts[h]
            d = _dot(qh, kh) + (_dot(qh, kl) + _dot(ql, kh))
            s = s + jnp.maximum(d * idx_scale, 0.0) * wi[:, h:h + 1]
        s = s + 0.0
        bits = pltpu.bitcast(s, jnp.int32)
        key = bits ^ ((bits >> 31) & jnp.int32(0x7FFFFFFF))
        key_ref[:, pl.ds(off, ck)] = jnp.where(off + kiota <= qpos, key, INT32_MIN)
        return carry

    lax.fori_loop(0, nch, score_chunk, 0)

    def count(pred):
        def body(c, acc):
            off = pl.multiple_of(c * ck, ck)
            m = jnp.where(pred(key_ref[:, pl.ds(off, ck)]), 1, 0)
            for j in range(ck // 128):
                acc = acc + m[:, j * 128:(j + 1) * 128]
            return acc
        acc = lax.fori_loop(0, nch, body, jnp.zeros((tq, 128), jnp.int32))
        return jnp.sum(acc, axis=1, keepdims=True)

    def bit_body(i, lo):
        cand = lo + (jnp.int32(1) << (31 - i))
        cnt = count(lambda k: k >= cand)
        return jnp.where(cnt >= topk, cand, lo)

    tau = lax.fori_loop(0, 32, bit_body, jnp.full((tq, 1), INT32_MIN, jnp.int32))
    need = (topk - count(lambda k: k > tau)).astype(F32)

    def mask_chunk(c, run):
        off = pl.multiple_of(c * ck, ck)
        k = key_ref[:, pl.ds(off, ck)]
        causal = off + kiota <= qpos
        eq = jnp.logical_and(k == tau, causal)
        eqf = jnp.where(eq, 1.0, 0.0)
        before = _dot(eqf.astype(BF16), tri_ref[...]) + run
        take = jnp.logical_or(k > tau, jnp.logical_and(eq, before < need))
        mask_ref[0, :, pl.ds(off, ck)] = jnp.where(take, 1, 0).astype(jnp.int8)
        return run + jnp.sum(eqf, axis=1, keepdims=True)

    lax.fori_loop(0, nch, mask_chunk, jnp.zeros((tq, 1), F32))

    def zero_chunk(c, carry):
        off = pl.multiple_of(c * ck, ck)
        mask_ref[0, :, pl.ds(off, ck)] = jnp.zeros((tq, ck), jnp.int8)
        return carry

    lax.fori_loop(nch, seq // ck, zero_chunk, 0)


def _dsa_mask(qi, wi, kit, topk):
    bsz, seq, _ = qi.shape
    tq, ck = DSA_TQ, DSA_CK
    tri = jnp.asarray(np.triu(np.ones((ck, ck), np.float32), 1), BF16)
    return pl.pallas_call(
        functools.partial(_dsa_mask_kernel, seq=seq, topk=topk),
        grid=(bsz, seq // tq),
        in_specs=[pl.BlockSpec((1, tq, IDX_HEADS * IDX_HEAD_DIM), lambda b, q: (b, q, 0)),
                  pl.BlockSpec((1, tq, IDX_HEADS), lambda b, q: (b, q, 0)),
                  pl.BlockSpec((1, IDX_HEAD_DIM, seq), lambda b, q: (b, 0, 0)),
                  pl.BlockSpec((ck, ck), lambda b, q: (0, 0))],
        out_specs=pl.BlockSpec((1, tq, seq), lambda b, q: (b, q, 0)),
        out_shape=jax.ShapeDtypeStruct((bsz, seq, seq), jnp.int8),
        scratch_shapes=[pltpu.VMEM((tq, seq), jnp.int32)],
        compiler_params=_cparams(("parallel", "parallel")),
        name="dsa_topk_mask",
    )(qi, wi, kit, tri)


ATT_TQ = 256
ATT_TK = 512


def _dsa_attn_kernel(q_ref, k_ref, v_ref, m_ref, o_ref, m_sc, acc_sc, bias_sc):
    tq, tk, hd = ATT_TQ, ATT_TK, ATT_HEAD_DIM
    qb = pl.program_id(1)
    kb = pl.program_id(2)

    @pl.when(kb == 0)
    def _():
        m_sc[...] = jnp.full(m_sc.shape, -jnp.inf, F32)
        acc_sc[...] = jnp.zeros(acc_sc.shape, F32)

    @pl.when(kb * tk <= qb * tq + (tq - 1))
    def _():
        bias_sc[...] = jnp.where(m_ref[0].astype(jnp.int32) != 0, 0.0, -jnp.inf)
        ones = jnp.ones((tk, hd), BF16)
        for h in range(ATT_HEADS):
            cs = slice(h * hd, (h + 1) * hd)
            ws = slice(2 * h * hd, 2 * (h + 1) * hd)
            s = _dot(q_ref[0, :, cs], k_ref[0, :, cs], _NT) + bias_sc[...]
            m_prev = m_sc[h]
            m_new = jnp.maximum(m_prev, jnp.max(s, axis=1, keepdims=True))
            m_use = jnp.where(m_new == -jnp.inf, 0.0, m_new)
            p = jnp.exp2(s - m_use).astype(BF16)
            alpha = jnp.exp2(m_prev - m_use)
            v_ext = jnp.concatenate([v_ref[0, :, cs], ones], axis=1)
            acc_sc[:, ws] = alpha * acc_sc[:, ws] + _dot(p, v_ext)
            m_sc[h] = m_new

    @pl.when(kb == pl.num_programs(2) - 1)
    def _():
        for h in range(ATT_HEADS):
            o_ref[0, :, h * hd:(h + 1) * hd] = (acc_sc[:, 2 * h * hd:(2 * h + 1) * hd]
                                               / acc_sc[:, (2 * h + 1) * hd:(2 * h + 2) * hd]).astype(o_ref.dtype)


def _dsa_attend(q, k, v, mask, out_dtype=BF16):
    bsz, seq, width = q.shape
    tq, tk = ATT_TQ, ATT_TK

    def kv_idx(b, i, j):
        return (b, jnp.minimum(j, (i * tq + tq - 1) // tk), 0)

    def m_idx(b, i, j):
        return (b, i, jnp.minimum(j, (i * tq + tq - 1) // tk))

    return pl.pallas_call(
        _dsa_attn_kernel,
        grid=(bsz, seq // tq, seq // tk),
        in_specs=[pl.BlockSpec((1, tq, width), lambda b, i, j: (b, i, 0)),
                  pl.BlockSpec((1, tk, width), kv_idx),
                  pl.BlockSpec((1, tk, width), kv_idx),
                  pl.BlockSpec((1, tq, tk), m_idx)],
        out_specs=pl.BlockSpec((1, tq, width), lambda b, i, j: (b, i, 0)),
        out_shape=jax.ShapeDtypeStruct((bsz, seq, width), out_dtype),
        scratch_shapes=[pltpu.VMEM((ATT_HEADS, tq, 1), F32), pltpu.VMEM((tq, 2 * width), F32),
                        pltpu.VMEM((tq, tk), F32)],
        compiler_params=_cparams(("parallel", "parallel", "arbitrary")),
        name="dsa_attention",
    )(q, k, v, mask)


def _rope_partial(x, pos, rot):
    half = rot // 2
    inv = ROPE_THETA ** (-jnp.arange(half, dtype=F32) * 2.0 / rot)
    ang = pos.astype(F32)[:, None] * inv[None, :]
    cos = jnp.cos(ang)[:, None, :]
    sin = jnp.sin(ang)[:, None, :]
    xr = x[..., :rot]
    x1, x2 = xr[..., :half], xr[..., half:]
    rotated = jnp.concatenate([x1 * cos - x2 * sin, x2 * cos + x1 * sin], axis=-1)
    return jnp.concatenate([rotated, x[..., rot:]], axis=-1)


def _dsa_branch(q, k, v, qi, ki, wi):
    bsz, seq, _ = q.shape
    pos = jnp.arange(seq)
    topk = min(TOPK_MAX, seq // 4)
    q = _rope_partial(q.reshape(bsz, seq, ATT_HEADS, ATT_HEAD_DIM), pos, ATT_ROT).reshape(bsz, seq, ATT_WIDTH)
    k = _rope_partial(k.reshape(bsz, seq, ATT_HEADS, ATT_HEAD_DIM), pos, ATT_ROT).reshape(bsz, seq, ATT_WIDTH)
    qi = _rope_partial(qi.reshape(bsz, seq, IDX_HEADS, IDX_HEAD_DIM), pos, IDX_ROT)
    qi = qi.reshape(bsz, seq, IDX_HEADS * IDX_HEAD_DIM)
    ki = _rope_partial(ki.reshape(bsz, seq, 1, IDX_HEAD_DIM), pos, IDX_ROT)[:, :, 0]
    wi = wi * (IDX_HEADS ** -0.5)
    mask = _dsa_mask(qi, wi, ki.swapaxes(1, 2), topk)
    qs = (q * (ATT_HEAD_DIM ** -0.5 * math.log2(math.e))).astype(BF16)
    return _dsa_attend(qs, k.astype(BF16), v.astype(BF16), mask)


RW_L = 64

_dot_pair = _dot_b
_dot_inv = _dot_b
_dot_app = _dot_b
_dot_state = _dot_b


def _rwkv_kernel(r_ref, lw_ref, k_ref, v_ref, a_ref, b_ref, tril_ref, y_ref, s_ref):
    L, hd = RW_L, RWKV_HEAD

    @pl.when(pl.program_id(1) == 0)
    def _():
        s_ref[...] = jnp.zeros(s_ref.shape, F32)

    lw = lw_ref[0]
    tril = tril_ref[...]
    h1 = lw.astype(BF16)
    r1 = lw - h1.astype(F32)
    h2 = r1.astype(BF16)
    h3 = (r1 - h2.astype(F32)).astype(BF16)
    cum = _dot(tril, h1) + (_dot(tril, h2) + _dot(tril, h3))
    tot = cum[L - 1:L, :]
    e_neg = jnp.exp(-cum)
    e_end = jnp.exp(tot - cum)
    w_tot = jnp.exp(tot)
    rt = r_ref[0] * jnp.exp(cum)
    at = a_ref[0] * jnp.exp(cum - lw)
    bt = b_ref[0] * e_neg
    kt = k_ref[0] * e_neg
    bh = b_ref[0] * e_end
    kh = k_ref[0] * e_end
    v = v_ref[0]

    row = lax.broadcasted_iota(jnp.int32, (2 * L, L), 0)
    col = lax.broadcasted_iota(jnp.int32, (2 * L, L), 1)
    keep = col < jnp.where(row < L, row, row - (L - 1))
    eye = lax.broadcasted_iota(jnp.int32, (L, L), 0) == lax.broadcasted_iota(jnp.int32, (L, L), 1)

    hs = range(RWKV_HEADS)
    cs = [slice(h * hd, (h + 1) * hd) for h in hs]
    lhs = [jnp.concatenate([at[:, c], rt[:, c]], axis=0) for c in cs]
    a_b = [jnp.where(keep, _dot_pair(lhs[h], bt[:, cs[h]], _NT), 0.0) for h in hs]
    a_k = [jnp.where(keep, _dot_pair(lhs[h], kt[:, cs[h]], _NT), 0.0) for h in hs]
    n = [a_b[h][:L] for h in hs]
    ident = jnp.where(eye, 1.0, 0.0)
    t_inv = [ident + n[h] for h in hs]
    for _ in range(int(math.log2(L)) - 1):
        n = [_dot_inv(n[h], n[h]) for h in hs]
        t_inv = [t_inv[h] + _dot_inv(t_inv[h], n[h]) for h in hs]
    akv = [_dot_app(a_k[h], v[:, cs[h]]) for h in hs]
    p12 = [_dot_app(t_inv[h], jnp.concatenate([at[:, cs[h]], akv[h][:L]], axis=1)) for h in hs]
    q12 = [_dot_app(a_b[h][L:], p12[h]) + jnp.concatenate([rt[:, cs[h]], akv[h][L:]], axis=1) for h in hs]
    s = [s_ref[h] for h in hs]
    y_ref[0] = jnp.concatenate([_dot_state(q12[h][:, :hd], s[h], _NT) + q12[h][:, hd:] for h in hs], axis=1)
    g = [_dot_app(p12[h], bh[:, cs[h]], _TN) for h in hs]
    vk = [_dot_app(v[:, cs[h]], kh[:, cs[h]], _TN) for h in hs]
    s_new = [s[h] * w_tot[:, cs[h]] + _dot_state(s[h], g[h][:hd]) + g[h][hd:] + vk[h] for h in hs]
    for h in hs:
        s_ref[h] = s_new[h]


def _rwkv_scan(r, lw, k, v, a_vec, b_vec):
    bsz, seq, width = r.shape
    L = RW_L
    tril = jnp.asarray(np.tril(np.ones((L, L), np.float32)), BF16)
    spec = pl.BlockSpec((1, L, width), lambda b, c: (b, c, 0))
    return pl.pallas_call(
        _rwkv_kernel,
        grid=(bsz, seq // L),
        in_specs=[spec] * 6 + [pl.BlockSpec((L, L), lambda b, c: (0, 0))],
        out_specs=spec,
        out_shape=jax.ShapeDtypeStruct((bsz, seq, width), F32),
        scratch_shapes=[pltpu.VMEM((RWKV_HEADS, RWKV_HEAD, RWKV_HEAD), F32)],
        compiler_params=_cparams(("parallel", "arbitrary")),
        name="rwkv7_scan",
    )(r, lw, k, v, a_vec, b_vec, tril)


def _rwkv_branch(p, mu, w0, w2, a0, a2, g2, k_k, k_a, r_k, lnx_g, lnx_b):
    bsz, seq, _ = p.shape
    n = bsz * seq
    prev = jnp.pad(p, ((0, 0), (1, 0), (0, 0)))[:, :-1]
    p = p + (prev - p) * mu
    r, k, v, wl, al, gl = jnp.split(p, np.cumsum([RWKV_WIDTH] * 3 + [DECAY_LORA, AAA_LORA]).tolist(), axis=-1)
    w = -jax.nn.softplus(-(w0 + _mm(jnp.tanh(wl).reshape(n, -1), w2).reshape(bsz, seq, -1))) - 0.5
    lw = -jnp.exp(w)
    a = jax.nn.sigmoid(a0 + _mm(al.reshape(n, -1), a2).reshape(bsz, seq, -1))
    g = _mm(jax.nn.sigmoid(gl).reshape(n, -1), g2).reshape(bsz, seq, -1)

    def heads(t):
        return t.reshape(bsz, seq, RWKV_HEADS, RWKV_HEAD)

    kk = heads(k * k_k)
    kk = kk / jnp.maximum(jnp.sqrt(jnp.sum(kk * kk, axis=-1, keepdims=True)), 1e-12)
    kk = kk.reshape(bsz, seq, RWKV_WIDTH)
    k = k * (1.0 + (a - 1.0) * k_a)
    y = heads(_rwkv_scan(r, lw, k, v, -kk, kk * a))
    mean = jnp.mean(y, axis=-1, keepdims=True)
    var = jnp.mean(jnp.square(y - mean), axis=-1, keepdims=True)
    y = ((y - mean) * lax.rsqrt(var + RWKV_GN_EPS)).reshape(bsz, seq, RWKV_WIDTH) * lnx_g + lnx_b
    bonus = (jnp.sum(heads(r) * heads(k) * r_k, axis=-1, keepdims=True) * heads(v)).reshape(bsz, seq, RWKV_WIDTH)
    return (y + bonus) * g


def _ssd_kernel(xdt_ref, ae_ref, at_ref, b_ref, c_ref, tril_ref, y_ref, h_ref):
    L, hd, ns = SSM_CHUNK, SSM_HEAD_DIM, SSM_STATE
    hg = SSM_HEADS // SSM_GROUPS
    gw = hg * hd

    @pl.when(pl.program_id(1) == 0)
    def _():
        h_ref[...] = jnp.zeros(h_ref.shape, F32)

    def cumsum3(mat, x, dims):
        h1 = x.astype(BF16)
        r1 = x - h1.astype(F32)
        h2 = r1.astype(BF16)
        h3 = (r1 - h2.astype(F32)).astype(BF16)
        if dims is _NN:
            return _dot(mat, h1) + (_dot(mat, h2) + _dot(mat, h3))
        return _dot(h1, mat, dims) + (_dot(h2, mat, dims) + _dot(h3, mat, dims))

    tril = tril_ref[...]
    a_cum = cumsum3(tril, ae_ref[0], _NN)
    a_cum_t = cumsum3(tril, at_ref[0], _NT)
    tot = a_cum[L - 1:L, :]
    e_in = jnp.exp(a_cum)
    xw = xdt_ref[0] * jnp.exp(tot - a_cum)
    e_tot = jnp.exp(tot)
    lower = lax.broadcasted_iota(jnp.int32, (L, L), 1) <= lax.broadcasted_iota(jnp.int32, (L, L), 0)

    for g in range(SSM_GROUPS):
        gs = slice(g * gw, (g + 1) * gw)
        bg = b_ref[0, :, g * ns:(g + 1) * ns]
        cg = c_ref[0, :, g * ns:(g + 1) * ns]
        cb = _dot_b(cg, bg, _NT)
        ht = h_ref[g]
        y_off = _dot_b(cg, ht) * e_in[:, gs]
        for j in range(hg):
            h = g * hg + j
            cs = slice(h * hd, (h + 1) * hd)
            seg = a_cum[:, h * hd:h * hd + 1] - a_cum_t[h:h + 1, :]
            m = cb * jnp.exp(jnp.where(lower, seg, -jnp.inf))
            y_ref[0, :, cs] = _dot_b(m, xdt_ref[0, :, cs]) + y_off[:, j * hd:(j + 1) * hd]
        h_ref[g] = ht * e_tot[:, gs] + _dot_b(bg, xw[:, gs], _TN)


def _ssd_scan(xdt, a_exp, a_t, bm, cm):
    bsz, seq, width = xdt.shape
    L = SSM_CHUNK
    gn = SSM_GROUPS * SSM_STATE
    tril = jnp.asarray(np.tril(np.ones((L, L), np.float32)), BF16)
    wide = pl.BlockSpec((1, L, width), lambda b, c: (b, c, 0))
    bc = pl.BlockSpec((1, L, gn), lambda b, c: (b, c, 0))
    return pl.pallas_call(
        _ssd_kernel,
        grid=(bsz, seq // L),
        in_specs=[wide, wide, pl.BlockSpec((1, SSM_HEADS, L), lambda b, c: (b, 0, c)), bc, bc,
                  pl.BlockSpec((L, L), lambda b, c: (0, 0))],
        out_specs=wide,
        out_shape=jax.ShapeDtypeStruct((bsz, seq, width), F32),
        scratch_shapes=[pltpu.VMEM((SSM_GROUPS, SSM_STATE, width // SSM_GROUPS), F32)],
        compiler_params=_cparams(("parallel", "arbitrary")),
        name="mamba2_ssd",
    )(xdt, a_exp, a_t, bm, cm, tril)


def _mamba_branch(p, conv_w, conv_b, dt_bias, a_log, d_skip, norm_g):
    bsz, seq, _ = p.shape
    z, xbc, dt = jnp.split(p, [SSM_INNER, SSM_INNER + SSM_CONV_CH], axis=-1)
    xpad = jnp.pad(xbc, ((0, 0), (SSM_CONV - 1, 0), (0, 0)))
    xbc = sum(xpad[:, j:j + seq] * conv_w[j] for j in range(SSM_CONV)) + conv_b
    xbc = jax.nn.silu(xbc)
    xs, bm, cm = jnp.split(xbc, [SSM_INNER, SSM_INNER + SSM_GROUPS * SSM_STATE], axis=-1)
    dt = jax.nn.softplus(dt + dt_bias)
    a = dt * (-jnp.exp(a_log))
    xh = xs.reshape(bsz, seq, SSM_HEADS, SSM_HEAD_DIM)
    xdt = (xh * dt[..., None]).reshape(bsz, seq, SSM_INNER)
    a_exp = jnp.repeat(a, SSM_HEAD_DIM, axis=-1)
    y = _ssd_scan(xdt, a_exp, a.swapaxes(1, 2), bm, cm).reshape(bsz, seq, SSM_HEADS, SSM_HEAD_DIM)
    y = y + xh * d_skip[:, None]
    y = y.reshape(bsz, seq, SSM_INNER) * jax.nn.silu(z)
    yg = y.reshape(bsz, seq, SSM_GROUPS, SSM_INNER // SSM_GROUPS)
    yg = yg * lax.rsqrt(jnp.mean(yg * yg, axis=-1, keepdims=True) + NORM_EPS)
    return yg.reshape(bsz, seq, SSM_INNER) * norm_g


A_SIZES = [ATT_WIDTH] * 3 + [IDX_HEADS * IDX_HEAD_DIM, IDX_HEAD_DIM, IDX_HEADS]
B_COLS = 3 * RWKV_WIDTH + DECAY_LORA + AAA_LORA + GATE_LORA
C_COLS = SSM_INNER + SSM_CONV_CH + SSM_HEADS
GATE_COLS = N_BRANCH * D_MODEL
LANE = 128


def _pad_cols(w, mult=LANE):
    pad = (-w.shape[1]) % mult
    return jnp.pad(w, ((0, 0), (0, pad))) if pad else w


def _layer(x, bsz, seq, pre_mix_g, w_in, w_o_att, rwkv_mu, rwkv_w0, rwkv_w2, rwkv_a0, rwkv_a2, rwkv_g2,
           rwkv_k_k, rwkv_k_a, rwkv_r_k, rwkv_lnx_g, rwkv_lnx_b, w_o_rwkv, ssm_conv_w, ssm_conv_b,
           ssm_dt_bias, ssm_a_log, ssm_d, ssm_norm_g, w_o_ssm, w_out, post_mix_g, pre_ffn_g,
           w_ff1, w_ff2, post_ffn_g):
    n = bsz * seq
    a_cols = sum(A_SIZES)
    offs = np.cumsum([a_cols, B_COLS, C_COLS]).tolist()
    wa, wb, wc, wg = jnp.split(w_in, offs, axis=1)
    parts = [_pad_cols(wa), _pad_cols(wb), _pad_cols(wc), wg]
    widths = [p.shape[1] for p in parts]
    h = _rmsnorm(x, pre_mix_g)
    proj = _mm(h, jnp.concatenate(parts, axis=1))
    o = np.cumsum([0] + widths).tolist()
    pa = proj[:, o[0]:o[0] + a_cols].reshape(bsz, seq, a_cols)
    pb = proj[:, o[1]:o[1] + B_COLS].reshape(bsz, seq, B_COLS)
    pc = proj[:, o[2]:o[2] + C_COLS].reshape(bsz, seq, C_COLS)
    pg = proj[:, o[3]:o[3] + GATE_COLS]

    q, k, v, qi, ki, wi = jnp.split(pa, np.cumsum(A_SIZES)[:-1].tolist(), axis=-1)
    y_a = _mm(_dsa_branch(q, k, v, qi, ki, wi).reshape(n, ATT_WIDTH), w_o_att)
    y_b = _mm(_rwkv_branch(pb, rwkv_mu, rwkv_w0, rwkv_w2, rwkv_a0, rwkv_a2, rwkv_g2, rwkv_k_k, rwkv_k_a,
                           rwkv_r_k, rwkv_lnx_g, rwkv_lnx_b).reshape(n, RWKV_WIDTH), w_o_rwkv)
    y_c = _mm(_mamba_branch(pc, ssm_conv_w, ssm_conv_b, ssm_dt_bias, ssm_a_log, ssm_d,
                            ssm_norm_g).reshape(n, SSM_INNER), w_o_ssm)
    gates = jax.nn.sigmoid(pg)
    merged = (gates[:, :D_MODEL] * y_a + gates[:, D_MODEL:2 * D_MODEL] * y_b + gates[:, 2 * D_MODEL:] * y_c)
    x = _mm_norm_res(merged, w_out, post_mix_g, x)
    hf = _rmsnorm(x, pre_ffn_g)
    f = _mm(hf, w_ff1, out_dtype=BF16, act="relu2")
    return _mm_norm_res(f, w_ff2, post_ffn_g, x)


def kernel(x, pre_mix_g, w_in, w_o_att, rwkv_mu, rwkv_w0, rwkv_w2, rwkv_a0, rwkv_a2, rwkv_g2, rwkv_k_k,
           rwkv_k_a, rwkv_r_k, rwkv_lnx_g, rwkv_lnx_b, w_o_rwkv, ssm_conv_w, ssm_conv_b, ssm_dt_bias,
           ssm_a_log, ssm_d, ssm_norm_g, w_o_ssm, w_out, post_mix_g, pre_ffn_g, w_ff1, w_ff2, post_ffn_g):
    bsz, seq, d = x.shape
    params = (pre_mix_g, w_in, w_o_att, rwkv_mu, rwkv_w0, rwkv_w2, rwkv_a0, rwkv_a2, rwkv_g2, rwkv_k_k,
              rwkv_k_a, rwkv_r_k, rwkv_lnx_g, rwkv_lnx_b, w_o_rwkv, ssm_conv_w, ssm_conv_b, ssm_dt_bias,
              ssm_a_log, ssm_d, ssm_norm_g, w_o_ssm, w_out, post_mix_g, pre_ffn_g, w_ff1, w_ff2, post_ffn_g)
    y = x.reshape(bsz * seq, d)
    for i in range(pre_mix_g.shape[0]):
        y = _layer(y, bsz, seq, *[p[i] for p in params])
    return y.reshape(bsz, seq, d)
```

```python
import functools
import math

import jax
import jax.numpy as jnp
import numpy as np
from jax import lax
from jax.experimental import pallas as pl
from jax.experimental.pallas import tpu as pltpu

F32 = jnp.float32
BF16 = jnp.bfloat16

D_MODEL = 1024
ATT_HEADS = 8
ATT_HEAD_DIM = 128
ATT_WIDTH = ATT_HEADS * ATT_HEAD_DIM
IDX_HEADS = 4
IDX_HEAD_DIM = 64
IDX_Q_WIDTH = IDX_HEADS * IDX_HEAD_DIM
TOPK_MAX = 256
ROPE_THETA = 500000.0
ATT_ROT = ATT_HEAD_DIM // 4
IDX_ROT = IDX_HEAD_DIM // 4
RWKV_HEAD = 64
RWKV_WIDTH = D_MODEL
RWKV_HEADS = RWKV_WIDTH // RWKV_HEAD
DECAY_LORA = 64
AAA_LORA = 64
GATE_LORA = 128
RWKV_GN_EPS = 64e-5
SSM_INNER = 2 * D_MODEL
SSM_HEAD_DIM = 64
SSM_HEADS = SSM_INNER // SSM_HEAD_DIM
SSM_GROUPS = 2
SSM_STATE = 128
SSM_CONV = 4
SSM_CHUNK = 128
SSM_CONV_CH = SSM_INNER + 2 * SSM_GROUPS * SSM_STATE
D_FF = 4 * D_MODEL
N_BRANCH = 3
NORM_EPS = 1e-6

A_SIZES = [ATT_WIDTH] * 3 + [IDX_Q_WIDTH, IDX_HEAD_DIM, IDX_HEADS]
A_COLS = sum(A_SIZES)
B_COLS = 3 * RWKV_WIDTH + DECAY_LORA + AAA_LORA + GATE_LORA
C_COLS = SSM_INNER + SSM_CONV_CH + SSM_HEADS
GATE_COLS = N_BRANCH * D_MODEL

V7X_VMEM_BYTES = 64 * 1024 * 1024
VMEM_LIMIT = 56 * 1024 * 1024
LANE = 128
SUBLANE = 8

IDX_COLS = 3 * LANE
C_COLS_PAD = 38 * LANE

INT32_MIN = np.int32(-(2 ** 31))


def _cparams(sem):
    return pltpu.CompilerParams(dimension_semantics=sem, vmem_limit_bytes=VMEM_LIMIT)


def _split2(a):
    hi = a.astype(BF16)
    lo = (a - hi.astype(F32)).astype(BF16)
    return hi, lo


def _split3(a):
    h1 = a.astype(BF16)
    r1 = a - h1.astype(F32)
    h2 = r1.astype(BF16)
    h3 = (r1 - h2.astype(F32)).astype(BF16)
    return h1, h2, h3


_NN = (((1,), (0,)), ((), ()))
_NT = (((1,), (1,)), ((), ()))
_TN = (((0,), (0,)), ((), ()))


def _dot(a, b, dims=_NN):
    return lax.dot_general(a, b, dims, preferred_element_type=F32)


def _dot_b(a, b, dims=_NN):
    return _dot(a.astype(BF16), b.astype(BF16), dims)


def _dot_exact_rhs(parts, mat, dims=_NN):
    out = _dot(parts[0], mat, dims)
    for p in parts[1:]:
        out = out + _dot(p, mat, dims)
    return out


def _softplus(x):
    return jnp.maximum(x, 0.0) + jnp.log(1.0 + jnp.exp(-jnp.abs(x)))


def _sigmoid(x):
    return 1.0 / (1.0 + jnp.exp(-x))


def _rms(y, g):
    return y * lax.rsqrt(jnp.mean(y * y, axis=-1, keepdims=True) + NORM_EPS) * g


def _rmsnorm_kernel(x_ref, g_ref, o_ref):
    o_ref[...] = _rms(x_ref[...], g_ref[...]).astype(o_ref.dtype)


def _rmsnorm(x, g, out_dtype=BF16, tm=1024):
    n, d = x.shape
    return pl.pallas_call(
        _rmsnorm_kernel,
        grid=(n // tm,),
        in_specs=[pl.BlockSpec((tm, d), lambda i: (i, 0)), pl.BlockSpec((1, d), lambda i: (0, 0))],
        out_specs=pl.BlockSpec((tm, d), lambda i: (i, 0)),
        out_shape=jax.ShapeDtypeStruct((n, d), out_dtype),
        compiler_params=_cparams(("parallel",)),
        name="rmsnorm",
    )(x, g.reshape(1, d))


def _col_tile(n, cap=2560):
    best = LANE
    for t in range(LANE, min(n, cap) + 1, LANE):
        if n % t == 0:
            best = t
    return best


def _row_tile(m, cap=1024):
    for t in (1024, 512, 256, 128):
        if t <= cap and m % t == 0:
            return t
    return m


def _mm_kernel(x_ref, w_ref, o_ref, *, act):
    y = _dot(x_ref[...], w_ref[...])
    if act == "relu2":
        y = jnp.square(jnp.maximum(y, 0.0))
    o_ref[...] = y.astype(o_ref.dtype)


def _mm(x, w, out_dtype=F32, act=None, tn_cap=2560):
    m, k = x.shape
    _, n = w.shape
    x = x.astype(BF16)
    w = w.astype(BF16)
    tm = _row_tile(m, 512 if jnp.dtype(out_dtype).itemsize == 4 else 1024)
    tn = _col_tile(n, tn_cap)
    return pl.pallas_call(
        functools.partial(_mm_kernel, act=act),
        grid=(n // tn, m // tm),
        in_specs=[pl.BlockSpec((tm, k), lambda j, i: (i, 0)), pl.BlockSpec((k, tn), lambda j, i: (0, j))],
        out_specs=pl.BlockSpec((tm, tn), lambda j, i: (i, j)),
        out_shape=jax.ShapeDtypeStruct((m, n), out_dtype),
        compiler_params=_cparams(("parallel", "parallel")),
        name="matmul",
    )(x, w)


def _mm_rope_kernel(x_ref, w_ref, c_ref, s1_ref, s2_ref, o_ref, *, shift):
    y = _dot(x_ref[...], w_ref[...])
    tw = c_ref.shape[-1]
    for g in range(y.shape[1] // LANE):
        yg = y[:, g * LANE:(g + 1) * LANE]
        ts = slice((g * LANE) % tw, (g * LANE) % tw + LANE)
        c, s1, s2 = c_ref[0, :, ts], s1_ref[0, :, ts], s2_ref[0, :, ts]
        out = yg * c + pltpu.roll(yg, shift, 1) * s1 + pltpu.roll(yg, LANE - shift, 1) * s2
        o_ref[:, g * LANE:(g + 1) * LANE] = out.astype(o_ref.dtype)


def _mm_rope(x, w, tabs, shift, out_dtype, tn, seq):
    m, k = x.shape
    _, n = w.shape
    x = x.astype(BF16)
    w = w.astype(BF16)
    tm = _row_tile(min(m, seq), 512)
    tpb = seq // tm
    tw = tabs[0].shape[-1]
    tspec = pl.BlockSpec((1, tm, tw), lambda j, i: (j, i % tpb, 0))
    return pl.pallas_call(
        functools.partial(_mm_rope_kernel, shift=shift),
        grid=(n // tn, m // tm),
        in_specs=[pl.BlockSpec((tm, k), lambda j, i: (i, 0)), pl.BlockSpec((k, tn), lambda j, i: (0, j)),
                  tspec, tspec, tspec],
        out_specs=pl.BlockSpec((tm, tn), lambda j, i: (i, j)),
        out_shape=jax.ShapeDtypeStruct((m, n), out_dtype),
        compiler_params=_cparams(("parallel", "parallel")),
        name="matmul_rope",
    )(x, w, *tabs)


def _rope_tables(seq, head, n_heads, rot, width, scale=1.0):
    half = rot // 2
    inv = ROPE_THETA ** (-np.arange(half, dtype=np.float32) * 2.0 / rot)
    ang = jnp.arange(seq, dtype=F32)[:, None] * jnp.asarray(inv, F32)[None, :]
    cos, sin = jnp.cos(ang), jnp.sin(ang)
    zeros = jnp.zeros((seq, head - rot), F32)
    c_head = jnp.concatenate([cos, cos, zeros + 1.0], axis=1)
    s1_head = jnp.concatenate([jnp.zeros_like(sin), sin, zeros], axis=1)
    s2_head = jnp.concatenate([-sin, jnp.zeros_like(sin), zeros], axis=1)
    rest = width - head * n_heads
    c = jnp.concatenate([c_head] * n_heads + [jnp.ones((seq, rest), F32)], axis=1)
    s1 = jnp.concatenate([s1_head] * n_heads + [jnp.zeros((seq, rest), F32)], axis=1)
    s2 = jnp.concatenate([s2_head] * n_heads + [jnp.zeros((seq, rest), F32)], axis=1)
    return c * scale, s1 * scale, s2 * scale


def _mm_norm_res_kernel(x_ref, w_ref, g_ref, r_ref, g2_ref, o_ref, h_ref):
    y = r_ref[...] + _rms(_dot(x_ref[...], w_ref[...]), g_ref[...])
    o_ref[...] = y
    h_ref[...] = _rms(y, g2_ref[...]).astype(h_ref.dtype)


def _mm_norm_res(x, w, g, res, g_next, tm=512):
    m, k = x.shape
    _, n = w.shape
    x = x.astype(BF16)
    w = w.astype(BF16)
    row = pl.BlockSpec((tm, n), lambda i: (i, 0))
    vec = pl.BlockSpec((1, n), lambda i: (0, 0))
    return pl.pallas_call(
        _mm_norm_res_kernel,
        grid=(m // tm,),
        in_specs=[pl.BlockSpec((tm, k), lambda i: (i, 0)), pl.BlockSpec((k, n), lambda i: (0, 0)), vec, row, vec],
        out_specs=[row, row],
        out_shape=[jax.ShapeDtypeStruct((m, n), F32), jax.ShapeDtypeStruct((m, n), BF16)],
        compiler_params=_cparams(("parallel",)),
        name="matmul_norm_residual",
    )(x, w, g.reshape(1, n), res, g_next.reshape(1, n))


DSA_TQ = 128
DSA_CK = 512


def _dsa_mask_kernel(idx_ref, kit_ref, tri_ref, mask_ref, key_ref, *, seq, topk):
    tq, ck = DSA_TQ, DSA_CK
    qb = pl.program_id(1)
    nch = qb // (ck // tq) + 1
    qpos = qb * tq + lax.broadcasted_iota(jnp.int32, (tq, ck), 0)
    kiota = lax.broadcasted_iota(jnp.int32, (tq, ck), 1)
    idx_scale = IDX_HEAD_DIM ** -0.5
    idx = idx_ref[0]
    wi0 = IDX_Q_WIDTH + IDX_HEAD_DIM
    q_parts = [_split2(idx[:, h * IDX_HEAD_DIM:(h + 1) * IDX_HEAD_DIM]) for h in range(IDX_HEADS)]

    def score_chunk(c, carry):
        off = pl.multiple_of(c * ck, ck)
        kh, kl = _split2(kit_ref[0, :, pl.ds(off, ck)])
        s = jnp.zeros((tq, ck), F32)
        for h in range(IDX_HEADS):
            qh, ql = q_parts[h]
            d = _dot(qh, kh) + (_dot(qh, kl) + _dot(ql, kh))
            s = s + jnp.maximum(d * idx_scale, 0.0) * idx[:, wi0 + h:wi0 + h + 1]
        s = s + 0.0
        bits = pltpu.bitcast(s, jnp.int32)
        key = bits ^ ((bits >> 31) & jnp.int32(0x7FFFFFFF))
        key_ref[:, pl.ds(off, ck)] = jnp.where(off + kiota <= qpos, key, INT32_MIN)
        return carry

    lax.fori_loop(0, nch, score_chunk, 0)

    def count(pred):
        def body(c, acc):
            off = pl.multiple_of(c * ck, ck)
            m = jnp.where(pred(key_ref[:, pl.ds(off, ck)]), 1, 0)
            for j in range(ck // LANE):
                acc = acc + m[:, j * LANE:(j + 1) * LANE]
            return acc
        acc = lax.fori_loop(0, nch, body, jnp.zeros((tq, LANE), jnp.int32))
        return jnp.sum(acc, axis=1, keepdims=True)

    def search_cond(st):
        i, _, done = st
        return jnp.logical_and(i < 32, jnp.min(done) == 0)

    def search_body(st):
        i, lo, done = st
        cand = lo + (jnp.int32(1) << (31 - i))
        cnt = count(lambda k: k >= cand)
        lo = jnp.where(jnp.logical_and(cnt >= topk, done == 0), cand, lo)
        return i + 1, lo, jnp.where(cnt == topk, 1, done)

    row_pos = qb * tq + lax.broadcasted_iota(jnp.int32, (tq, 1), 0)
    done0 = jnp.where(row_pos + 1 <= topk, 1, 0)
    _, tau, _ = lax.while_loop(search_cond, search_body,
                               (jnp.int32(0), jnp.full((tq, 1), INT32_MIN, jnp.int32), done0))
    need = (topk - count(lambda k: k > tau)).astype(F32)

    def mask_chunk(c, run):
        off = pl.multiple_of(c * ck, ck)
        k = key_ref[:, pl.ds(off, ck)]
        causal = off + kiota <= qpos
        eq = jnp.logical_and(k == tau, causal)
        eqf = jnp.where(eq, 1.0, 0.0)
        before = _dot(eqf.astype(BF16), tri_ref[...]) + run
        take = jnp.logical_or(k > tau, jnp.logical_and(eq, before < need))
        mask_ref[0, :, pl.ds(off, ck)] = jnp.where(take, 1, 0).astype(jnp.int8)
        return run + jnp.sum(eqf, axis=1, keepdims=True)

    lax.fori_loop(0, nch, mask_chunk, jnp.zeros((tq, 1), F32))

    def zero_chunk(c, carry):
        off = pl.multiple_of(c * ck, ck)
        mask_ref[0, :, pl.ds(off, ck)] = jnp.zeros((tq, ck), jnp.int8)
        return carry

    lax.fori_loop(nch, seq // ck, zero_chunk, 0)


def _dsa_mask(idx, kit, topk):
    bsz, seq, _ = idx.shape
    tq, ck = DSA_TQ, DSA_CK
    tri = jnp.asarray(np.triu(np.ones((ck, ck), np.float32), 1), BF16)
    return pl.pallas_call(
        functools.partial(_dsa_mask_kernel, seq=seq, topk=topk),
        grid=(bsz, seq // tq),
        in_specs=[pl.BlockSpec((1, tq, IDX_COLS), lambda b, q: (b, q, 0)),
                  pl.BlockSpec((1, IDX_HEAD_DIM, seq), lambda b, q: (b, 0, 0)),
                  pl.BlockSpec((ck, ck), lambda b, q: (0, 0))],
        out_specs=pl.BlockSpec((1, tq, seq), lambda b, q: (b, q, 0)),
        out_shape=jax.ShapeDtypeStruct((bsz, seq, seq), jnp.int8),
        scratch_shapes=[pltpu.VMEM((tq, seq), jnp.int32)],
        compiler_params=_cparams(("parallel", "parallel")),
        name="dsa_topk_mask",
    )(idx, kit, tri)


ATT_TQ = 256
ATT_TK = 512


def _dsa_attn_kernel(q_ref, k_ref, v_ref, m_ref, o_ref, m_sc, acc_sc, bias_sc):
    tq, tk, hd = ATT_TQ, ATT_TK, ATT_HEAD_DIM
    qb = pl.program_id(1)
    kb = pl.program_id(2)

    @pl.when(kb == 0)
    def _():
        m_sc[...] = jnp.full(m_sc.shape, -jnp.inf, F32)
        acc_sc[...] = jnp.zeros(acc_sc.shape, F32)

    @pl.when(kb * tk <= qb * tq + (tq - 1))
    def _():
        bias_sc[...] = jnp.where(m_ref[0].astype(jnp.int32) != 0, 0.0, -jnp.inf)
        ones = jnp.ones((tk, hd), BF16)
        for h in range(ATT_HEADS):
            cs = slice(h * hd, (h + 1) * hd)
            ws = slice(2 * h * hd, 2 * (h + 1) * hd)
            s = _dot(q_ref[0, :, cs], k_ref[0, :, cs], _NT) + bias_sc[...]
            m_prev = m_sc[h]
            m_new = jnp.maximum(m_prev, jnp.max(s, axis=1, keepdims=True))
            m_use = jnp.where(m_new == -jnp.inf, 0.0, m_new)
            p = jnp.exp2(s - m_use).astype(BF16)
            alpha = jnp.exp2(m_prev - m_use)
            v_ext = jnp.concatenate([v_ref[0, :, cs], ones], axis=1)
            acc_sc[:, ws] = alpha * acc_sc[:, ws] + _dot(p, v_ext)
            m_sc[h] = m_new

    @pl.when(kb == pl.num_programs(2) - 1)
    def _():
        for h in range(ATT_HEADS):
            o_ref[0, :, h * hd:(h + 1) * hd] = (acc_sc[:, 2 * h * hd:(2 * h + 1) * hd]
                                               / acc_sc[:, (2 * h + 1) * hd:(2 * h + 2) * hd]).astype(o_ref.dtype)


def _dsa_attend(qk, v, mask, out_dtype=BF16):
    bsz, seq, width = v.shape
    tq, tk = ATT_TQ, ATT_TK

    def last_kb(i):
        return (i * tq + tq - 1) // tk

    return pl.pallas_call(
        _dsa_attn_kernel,
        grid=(bsz, seq // tq, seq // tk),
        in_specs=[pl.BlockSpec((1, tq, width), lambda b, i, j: (b, i, 0)),
                  pl.BlockSpec((1, tk, width), lambda b, i, j: (b, jnp.minimum(j, last_kb(i)), 1)),
                  pl.BlockSpec((1, tk, width), lambda b, i, j: (b, jnp.minimum(j, last_kb(i)), 0)),
                  pl.BlockSpec((1, tq, tk), lambda b, i, j: (b, i, jnp.minimum(j, last_kb(i))))],
        out_specs=pl.BlockSpec((1, tq, width), lambda b, i, j: (b, i, 0)),
        out_shape=jax.ShapeDtypeStruct((bsz, seq, width), out_dtype),
        scratch_shapes=[pltpu.VMEM((ATT_HEADS, tq, 1), F32), pltpu.VMEM((tq, 2 * width), F32),
                        pltpu.VMEM((tq, tk), F32)],
        compiler_params=_cparams(("parallel", "parallel", "arbitrary")),
        name="dsa_attention",
    )(qk, qk, v, mask)


RW_L = 64
RW_SEG = 256


def _head_sums(x, bd_ref):
    rows = x.shape[0]
    hi, lo = _split2(x)
    st = jnp.concatenate([hi, lo], axis=0)
    y = jnp.concatenate([_dot(st[:, g * RW_SEG:(g + 1) * RW_SEG], bd_ref[...])
                         for g in range(RWKV_WIDTH // RW_SEG)], axis=1)
    return y[:rows] + y[rows:]


def _rwkv_kernel(p_ref, halo_ref, mu_ref, w0_ref, w2_ref, a0_ref, a2_ref, g2_ref, kk_ref, ka_ref, rk_ref,
                 lng_ref, lnb_ref, tril_ref, bd_ref, o_ref, s_ref):
    L, hd, W = RW_L, RWKV_HEAD, RWKV_WIDTH
    c = pl.program_id(1)

    @pl.when(c == 0)
    def _():
        s_ref[...] = jnp.zeros(s_ref.shape, F32)

    p = p_ref[0]
    first = halo_ref[0, SUBLANE - 1:SUBLANE, :] * jnp.where(c > 0, 1.0, 0.0)
    row = lax.broadcasted_iota(jnp.int32, p.shape, 0)
    prev = jnp.where(row == 0, first, pltpu.roll(p, 1, 0))
    p = p + (prev - p) * mu_ref[...]
    r = p[:, 0:W]
    k = p[:, W:2 * W]
    v = p[:, 2 * W:3 * W]
    o_lora = 3 * W
    wl = p[:, o_lora:o_lora + DECAY_LORA]
    al = p[:, o_lora + DECAY_LORA:o_lora + DECAY_LORA + AAA_LORA]
    gl = p[:, o_lora + DECAY_LORA + AAA_LORA:]
    w = -_softplus(-(w0_ref[...] + _dot(jnp.tanh(wl).astype(BF16), w2_ref[...]))) - 0.5
    lw = -jnp.exp(w)
    a = _sigmoid(a0_ref[...] + _dot(al.astype(BF16), a2_ref[...]))
    gate = _dot(_sigmoid(gl).astype(BF16), g2_ref[...])
    khat = k * kk_ref[...]
    kk = khat / jnp.maximum(jnp.sqrt(_head_sums(khat * khat, bd_ref)), 1e-12)
    k = k * (1.0 + (a - 1.0) * ka_ref[...])
    av = -kk
    bv = kk * a

    tril = tril_ref[...]
    l1, l2, l3 = _split3(lw)
    cum = _dot(tril, l1) + (_dot(tril, l2) + _dot(tril, l3))
    tot = cum[L - 1:L, :]
    e_neg = jnp.exp(-cum)
    e_end = jnp.exp(tot - cum)
    w_tot = jnp.exp(tot)
    rt = r * jnp.exp(cum)
    at = av * jnp.exp(cum - lw)
    bt = bv * e_neg
    kt = k * e_neg
    bh = bv * e_end
    kh = k * e_end

    rw = lax.broadcasted_iota(jnp.int32, (2 * L, L), 0)
    cl = lax.broadcasted_iota(jnp.int32, (2 * L, L), 1)
    keep = cl < jnp.where(rw < L, rw, rw - (L - 1))
    eye = lax.broadcasted_iota(jnp.int32, (L, L), 0) == lax.broadcasted_iota(jnp.int32, (L, L), 1)

    hs = range(RWKV_HEADS)
    cs = [slice(h * hd, (h + 1) * hd) for h in hs]
    lhs = [jnp.concatenate([at[:, s], rt[:, s]], axis=0) for s in cs]
    a_b = [jnp.where(keep, _dot_b(lhs[h], bt[:, cs[h]], _NT), 0.0) for h in hs]
    a_k = [jnp.where(keep, _dot_b(lhs[h], kt[:, cs[h]], _NT), 0.0) for h in hs]
    n = [a_b[h][:L] for h in hs]
    ident = jnp.where(eye, 1.0, 0.0)
    t_inv = [ident + n[h] for h in hs]
    for _ in range(int(math.log2(L)) - 1):
        n = [_dot_b(n[h], n[h]) for h in hs]
        t_inv = [t_inv[h] + _dot_b(t_inv[h], n[h]) for h in hs]
    akv = [_dot_b(a_k[h], v[:, cs[h]]) for h in hs]
    p12 = [_dot_b(t_inv[h], jnp.concatenate([at[:, cs[h]], akv[h][:L]], axis=1)) for h in hs]
    q12 = [_dot_b(a_b[h][L:], p12[h]) + jnp.concatenate([rt[:, cs[h]], akv[h][L:]], axis=1) for h in hs]
    s = [s_ref[h] for h in hs]
    y = jnp.concatenate([_dot_b(q12[h][:, :hd], s[h], _NT) + q12[h][:, hd:] for h in hs], axis=1)
    g = [_dot_b(p12[h], bh[:, cs[h]], _TN) for h in hs]
    vk = [_dot_b(v[:, cs[h]], kh[:, cs[h]], _TN) for h in hs]
    s_new = [s[h] * w_tot[:, cs[h]] + _dot_b(s[h], g[h][:hd]) + g[h][hd:] + vk[h] for h in hs]
    for h in hs:
        s_ref[h] = s_new[h]

    inv_hd = 1.0 / hd
    mean = _head_sums(y, bd_ref) * inv_hd
    yc = y - mean
    var = _head_sums(yc * yc, bd_ref) * inv_hd
    yn = yc * lax.rsqrt(var + RWKV_GN_EPS) * lng_ref[...] + lnb_ref[...]
    bonus = _head_sums(r * k * rk_ref[...], bd_ref) * v
    o_ref[0] = ((yn + bonus) * gate).astype(o_ref.dtype)


def _rwkv_branch(pb, mu, w0, w2, a0, a2, g2, k_k, k_a, r_k, lnx_g, lnx_b, out_dtype=BF16):
    bsz, seq, cols = pb.shape
    L, W = RW_L, RWKV_WIDTH
    tril = jnp.asarray(np.tril(np.ones((L, L), np.float32)), BF16)
    seg = np.arange(RW_SEG) // RWKV_HEAD
    bd = jnp.asarray((seg[:, None] == seg[None, :]).astype(np.float32), BF16)

    def vec(a):
        return a.reshape(1, -1).astype(F32)

    def full(shape):
        return pl.BlockSpec(shape, lambda b, c: (0,) * len(shape))

    return pl.pallas_call(
        _rwkv_kernel,
        grid=(bsz, seq // L),
        in_specs=[pl.BlockSpec((1, L, cols), lambda b, c: (b, c, 0)),
                  pl.BlockSpec((1, SUBLANE, cols), lambda b, c: (b, jnp.maximum(c * (L // SUBLANE) - 1, 0), 0)),
                  full((1, cols)), full((1, W)), full((DECAY_LORA, W)), full((1, W)), full((AAA_LORA, W)),
                  full((GATE_LORA, W)), full((1, W)), full((1, W)), full((1, W)), full((1, W)), full((1, W)),
                  full((L, L)), full((RW_SEG, RW_SEG))],
        out_specs=pl.BlockSpec((1, L, W), lambda b, c: (b, c, 0)),
        out_shape=jax.ShapeDtypeStruct((bsz, seq, W), out_dtype),
        scratch_shapes=[pltpu.VMEM((RWKV_HEADS, RWKV_HEAD, RWKV_HEAD), F32)],
        compiler_params=_cparams(("parallel", "arbitrary")),
        name="rwkv7_time_mix",
    )(pb, pb, vec(mu), vec(w0), w2.astype(BF16), vec(a0), a2.astype(BF16), g2.astype(BF16), vec(k_k), vec(k_a),
      vec(r_k), vec(lnx_g), vec(lnx_b), tril, bd)


def _mamba_kernel(p_ref, halo_ref, cw_ref, cb_ref, dtb_ref, alog_ref, dexp_ref, ng_ref, e_ref, triu_ref,
                  o_ref, h_ref, y_sc):
    L, hd, ns = SSM_CHUNK, SSM_HEAD_DIM, SSM_STATE
    hg = SSM_HEADS // SSM_GROUPS
    gw = hg * hd
    c = pl.program_id(1)

    @pl.when(c == 0)
    def _():
        h_ref[...] = jnp.zeros(h_ref.shape, F32)

    x0 = SSM_INNER
    xr = p_ref[0, :, x0:x0 + SSM_CONV_CH]
    halo = halo_ref[0, :, x0:x0 + SSM_CONV_CH] * jnp.where(c > 0, 1.0, 0.0)
    row8 = lax.broadcasted_iota(jnp.int32, halo.shape, 0)
    acc = xr * cw_ref[SSM_CONV - 1:SSM_CONV, :] + cb_ref[...]
    for s in range(1, SSM_CONV):
        rolled = pltpu.roll(xr, s, 0)
        top = jnp.where(row8 < s, pltpu.roll(halo, s, 0), rolled[0:SUBLANE])
        shifted = jnp.concatenate([top, rolled[SUBLANE:]], axis=0)
        acc = acc + shifted * cw_ref[SSM_CONV - 1 - s:SSM_CONV - s, :]
    xbc = acc * _sigmoid(acc)
    xs = xbc[:, :SSM_INNER]
    bm = xbc[:, SSM_INNER:SSM_INNER + SSM_GROUPS * ns]
    cm = xbc[:, SSM_INNER + SSM_GROUPS * ns:]

    dt = _softplus(p_ref[0, :, x0 + SSM_CONV_CH:x0 + SSM_CONV_CH + SSM_HEADS] + dtb_ref[...])
    a = dt * (-jnp.exp(alog_ref[...]))
    triu = triu_ref[...]
    a_cum_t = _dot_exact_rhs(_split3(a), triu, _TN)
    a_cum_e = _dot_exact_rhs(_split3(a_cum_t), e_ref[...], _TN)
    dt_e = _dot_exact_rhs(_split3(dt), e_ref[...])
    xdt = xs * dt_e
    tot = a_cum_e[L - 1:L, :]
    e_in = jnp.exp(a_cum_e)
    xw = xdt * jnp.exp(tot - a_cum_e)
    e_tot = jnp.exp(tot)
    lower = lax.broadcasted_iota(jnp.int32, (L, L), 1) <= lax.broadcasted_iota(jnp.int32, (L, L), 0)

    for g in range(SSM_GROUPS):
        gs = slice(g * gw, (g + 1) * gw)
        bg = bm[:, g * ns:(g + 1) * ns]
        cg = cm[:, g * ns:(g + 1) * ns]
        cb = _dot_b(cg, bg, _NT)
        ht = h_ref[g]
        y_off = _dot_b(cg, ht) * e_in[:, gs]
        for j in range(hg):
            h = g * hg + j
            hsl = slice(h * hd, (h + 1) * hd)
            seg = a_cum_e[:, h * hd:h * hd + 1] - a_cum_t[h:h + 1, :]
            m = cb * jnp.exp(jnp.where(lower, seg, -jnp.inf))
            y_sc[:, hsl] = _dot_b(m, xdt[:, hsl]) + y_off[:, j * hd:(j + 1) * hd]
        h_ref[g] = ht * e_tot[:, gs] + _dot_b(bg, xw[:, gs], _TN)

    z = p_ref[0, :, 0:SSM_INNER]
    y = (y_sc[...] + xs * dexp_ref[...]) * (z * _sigmoid(z))
    for g in range(SSM_GROUPS):
        gs = slice(g * gw, (g + 1) * gw)
        o_ref[0, :, gs] = _rms(y[:, gs], ng_ref[:, gs]).astype(o_ref.dtype)


def _mamba_branch(pc, conv_w, conv_b, dt_bias, a_log, d_skip, norm_g, out_dtype=BF16):
    bsz, seq, cols = pc.shape
    L = SSM_CHUNK
    triu = jnp.asarray(np.triu(np.ones((L, L), np.float32)), BF16)
    expand = jnp.asarray(np.repeat(np.eye(SSM_HEADS, dtype=np.float32), SSM_HEAD_DIM, axis=1), BF16)

    def full(shape):
        return pl.BlockSpec(shape, lambda b, c: (0,) * len(shape))

    return pl.pallas_call(
        _mamba_kernel,
        grid=(bsz, seq // L),
        in_specs=[pl.BlockSpec((1, L, cols), lambda b, c: (b, c, 0)),
                  pl.BlockSpec((1, SUBLANE, cols), lambda b, c: (b, jnp.maximum(c * (L // SUBLANE) - 1, 0), 0)),
                  full((SSM_CONV, SSM_CONV_CH)), full((1, SSM_CONV_CH)), full((1, SSM_HEADS)), full((1, SSM_HEADS)),
                  full((1, SSM_INNER)), full((1, SSM_INNER)), full((SSM_HEADS, SSM_INNER)), full((L, L))],
        out_specs=pl.BlockSpec((1, L, SSM_INNER), lambda b, c: (b, c, 0)),
        out_shape=jax.ShapeDtypeStruct((bsz, seq, SSM_INNER), out_dtype),
        scratch_shapes=[pltpu.VMEM((SSM_GROUPS, SSM_STATE, SSM_INNER // SSM_GROUPS), F32),
                        pltpu.VMEM((L, SSM_INNER), F32)],
        compiler_params=_cparams(("parallel", "arbitrary")),
        name="mamba2_mixer",
    )(pc, pc, conv_w, conv_b.reshape(1, -1), dt_bias.reshape(1, -1), a_log.reshape(1, -1),
      jnp.repeat(d_skip, SSM_HEAD_DIM).reshape(1, -1), norm_g.reshape(1, -1), expand, triu)


def _merge_kernel(oa_ref, ob_ref, oc_ref, pg_ref, x_ref, wa_ref, wb_ref, wc_ref, wo_ref, g1_ref, g2_ref,
                  xo_ref, hf_ref):
    d = D_MODEL
    merged = (_sigmoid(pg_ref[:, 0:d].astype(F32)) * _dot(oa_ref[...], wa_ref[...])
              + _sigmoid(pg_ref[:, d:2 * d].astype(F32)) * _dot(ob_ref[...], wb_ref[...])
              + _sigmoid(pg_ref[:, 2 * d:3 * d].astype(F32)) * _dot(oc_ref[...], wc_ref[...]))
    x = x_ref[...] + _rms(_dot(merged.astype(BF16), wo_ref[...]), g1_ref[...])
    xo_ref[...] = x
    hf_ref[...] = _rms(x, g2_ref[...]).astype(hf_ref.dtype)


def _merge_out(o_a, o_b, o_c, pg, x, w_o_att, w_o_rwkv, w_o_ssm, w_out, post_mix_g, pre_ffn_g, tm=256):
    n, d = x.shape

    def row(width):
        return pl.BlockSpec((tm, width), lambda i: (i, 0))

    def full(shape):
        return pl.BlockSpec(shape, lambda i: (0, 0))

    return pl.pallas_call(
        _merge_kernel,
        grid=(n // tm,),
        in_specs=[row(ATT_WIDTH), row(RWKV_WIDTH), row(SSM_INNER), row(GATE_COLS), row(d),
                  full((ATT_WIDTH, d)), full((RWKV_WIDTH, d)), full((SSM_INNER, d)), full((d, d)),
                  full((1, d)), full((1, d))],
        out_specs=[row(d), row(d)],
        out_shape=[jax.ShapeDtypeStruct((n, d), F32), jax.ShapeDtypeStruct((n, d), BF16)],
        compiler_params=_cparams(("parallel",)),
        name="merge_out",
    )(o_a, o_b, o_c, pg, x, w_o_att.astype(BF16), w_o_rwkv.astype(BF16), w_o_ssm.astype(BF16),
      w_out.astype(BF16), post_mix_g.reshape(1, d), pre_ffn_g.reshape(1, d))


def _pad_cols(w, width):
    return jnp.pad(w, ((0, 0), (0, width - w.shape[1])))


def _layer(x, h, bsz, seq, g_next, w_in, w_o_att, rwkv_mu, rwkv_w0, rwkv_w2, rwkv_a0, rwkv_a2, rwkv_g2,
           rwkv_k_k, rwkv_k_a, rwkv_r_k, rwkv_lnx_g, rwkv_lnx_b, w_o_rwkv, ssm_conv_w, ssm_conv_b,
           ssm_dt_bias, ssm_a_log, ssm_d, ssm_norm_g, w_o_ssm, w_out, post_mix_g, pre_ffn_g,
           w_ff1, w_ff2, post_ffn_g):
    n = bsz * seq
    topk = min(TOPK_MAX, seq // 4)
    wa, wb, wc, wg = jnp.split(w_in, np.cumsum([A_COLS, B_COLS, C_COLS]).tolist(), axis=1)

    q_scale = ATT_HEAD_DIM ** -0.5 * math.log2(math.e)
    tq_ = _rope_tables(seq, ATT_HEAD_DIM, 1, ATT_ROT, LANE, q_scale)
    tk_ = _rope_tables(seq, ATT_HEAD_DIM, 1, ATT_ROT, LANE)
    qk = _mm_rope(h, wa[:, :2 * ATT_WIDTH], tuple(jnp.stack([a, b]) for a, b in zip(tq_, tk_)), ATT_ROT // 2,
                  BF16, ATT_WIDTH, seq).reshape(bsz, seq, 2 * ATT_WIDTH)
    v = _mm(h, wa[:, 2 * ATT_WIDTH:3 * ATT_WIDTH], BF16).reshape(bsz, seq, ATT_WIDTH)
    ti = list(_rope_tables(seq, IDX_HEAD_DIM, IDX_HEADS + 1, IDX_ROT, IDX_COLS))
    wi0 = IDX_Q_WIDTH + IDX_HEAD_DIM
    ti[0] = ti[0].at[:, wi0:wi0 + IDX_HEADS].set(IDX_HEADS ** -0.5)
    idx = _mm_rope(h, _pad_cols(wa[:, 3 * ATT_WIDTH:], IDX_COLS), tuple(t[None] for t in ti), IDX_ROT // 2,
                   F32, IDX_COLS, seq).reshape(bsz, seq, IDX_COLS)
    mask = _dsa_mask(idx, idx[:, :, IDX_Q_WIDTH:wi0].swapaxes(1, 2), topk)
    o_a = _dsa_attend(qk, v, mask).reshape(n, ATT_WIDTH)

    pb = _mm(h, wb, F32).reshape(bsz, seq, B_COLS)
    o_b = _rwkv_branch(pb, rwkv_mu, rwkv_w0, rwkv_w2, rwkv_a0, rwkv_a2, rwkv_g2, rwkv_k_k, rwkv_k_a, rwkv_r_k,
                       rwkv_lnx_g, rwkv_lnx_b).reshape(n, RWKV_WIDTH)
    pc = _mm(h, _pad_cols(wc, C_COLS_PAD), F32).reshape(bsz, seq, C_COLS_PAD)
    o_c = _mamba_branch(pc, ssm_conv_w, ssm_conv_b, ssm_dt_bias, ssm_a_log, ssm_d, ssm_norm_g).reshape(n, SSM_INNER)

    pg = _mm(h, wg, BF16)
    x, hf = _merge_out(o_a, o_b, o_c, pg, x, w_o_att, w_o_rwkv, w_o_ssm, w_out, post_mix_g, pre_ffn_g)
    f = _mm(hf, w_ff1, out_dtype=BF16, act="relu2")
    return _mm_norm_res(f, w_ff2, post_ffn_g, x, g_next)


def kernel(x, pre_mix_g, w_in, w_o_att, rwkv_mu, rwkv_w0, rwkv_w2, rwkv_a0, rwkv_a2, rwkv_g2, rwkv_k_k,
           rwkv_k_a, rwkv_r_k, rwkv_lnx_g, rwkv_lnx_b, w_o_rwkv, ssm_conv_w, ssm_conv_b, ssm_dt_bias,
           ssm_a_log, ssm_d, ssm_norm_g, w_o_ssm, w_out, post_mix_g, pre_ffn_g, w_ff1, w_ff2, post_ffn_g):
    bsz, seq, d = x.shape
    depth = pre_mix_g.shape[0]
    params = (w_in, w_o_att, rwkv_mu, rwkv_w0, rwkv_w2, rwkv_a0, rwkv_a2, rwkv_g2, rwkv_k_k,
              rwkv_k_a, rwkv_r_k, rwkv_lnx_g, rwkv_lnx_b, w_o_rwkv, ssm_conv_w, ssm_conv_b, ssm_dt_bias,
              ssm_a_log, ssm_d, ssm_norm_g, w_o_ssm, w_out, post_mix_g, pre_ffn_g, w_ff1, w_ff2, post_ffn_g)
    y = x.reshape(bsz * seq, d)
    h = _rmsnorm(y, pre_mix_g[0])
    for i in range(depth):
        g_next = pre_mix_g[min(i + 1, depth - 1)]
        y, h = _layer(y, h, bsz, seq, g_next, *[p[i] for p in params])
    return y.reshape(bsz, seq, d)
```

```python
import functools
import math

import jax
import jax.numpy as jnp
import numpy as np
from jax import lax
from jax.experimental import pallas as pl
from jax.experimental.pallas import tpu as pltpu

F32 = jnp.float32
BF16 = jnp.bfloat16

D_MODEL = 1024
ATT_HEADS = 8
ATT_HEAD_DIM = 128
ATT_WIDTH = ATT_HEADS * ATT_HEAD_DIM
IDX_HEADS = 4
IDX_HEAD_DIM = 64
IDX_Q_WIDTH = IDX_HEADS * IDX_HEAD_DIM
TOPK_MAX = 256
ROPE_THETA = 500000.0
ATT_ROT = ATT_HEAD_DIM // 4
IDX_ROT = IDX_HEAD_DIM // 4
RWKV_HEAD = 64
RWKV_WIDTH = D_MODEL
RWKV_HEADS = RWKV_WIDTH // RWKV_HEAD
DECAY_LORA = 64
AAA_LORA = 64
GATE_LORA = 128
RWKV_GN_EPS = 64e-5
SSM_INNER = 2 * D_MODEL
SSM_HEAD_DIM = 64
SSM_HEADS = SSM_INNER // SSM_HEAD_DIM
SSM_GROUPS = 2
SSM_STATE = 128
SSM_CONV = 4
SSM_CHUNK = 128
SSM_CONV_CH = SSM_INNER + 2 * SSM_GROUPS * SSM_STATE
D_FF = 4 * D_MODEL
N_BRANCH = 3
NORM_EPS = 1e-6

A_SIZES = [ATT_WIDTH] * 3 + [IDX_Q_WIDTH, IDX_HEAD_DIM, IDX_HEADS]
A_COLS = sum(A_SIZES)
B_COLS = 3 * RWKV_WIDTH + DECAY_LORA + AAA_LORA + GATE_LORA
C_COLS = SSM_INNER + SSM_CONV_CH + SSM_HEADS
GATE_COLS = N_BRANCH * D_MODEL

V7X_VMEM_BYTES = 64 * 1024 * 1024
VMEM_LIMIT = 56 * 1024 * 1024
LANE = 128
SUBLANE = 8

IDX_COLS = 3 * LANE
C_COLS_PAD = 38 * LANE

INT32_MIN = np.int32(-(2 ** 31))


def _cparams(sem):
    return pltpu.CompilerParams(dimension_semantics=sem, vmem_limit_bytes=VMEM_LIMIT)


def _split2(a):
    hi = a.astype(BF16)
    lo = (a - hi.astype(F32)).astype(BF16)
    return hi, lo


def _split3(a):
    h1 = a.astype(BF16)
    r1 = a - h1.astype(F32)
    h2 = r1.astype(BF16)
    h3 = (r1 - h2.astype(F32)).astype(BF16)
    return h1, h2, h3


_NN = (((1,), (0,)), ((), ()))
_NT = (((1,), (1,)), ((), ()))
_TN = (((0,), (0,)), ((), ()))


def _dot(a, b, dims=_NN):
    return lax.dot_general(a, b, dims, preferred_element_type=F32)


def _dot_b(a, b, dims=_NN):
    return _dot(a.astype(BF16), b.astype(BF16), dims)


def _dot_exact_rhs(parts, mat, dims=_NN):
    out = _dot(parts[0], mat, dims)
    for p in parts[1:]:
        out = out + _dot(p, mat, dims)
    return out


def _softplus(x):
    return jnp.maximum(x, 0.0) + jnp.log(1.0 + jnp.exp(-jnp.abs(x)))


def _sigmoid(x):
    return 1.0 / (1.0 + jnp.exp(-x))


def _rms(y, g):
    return y * lax.rsqrt(jnp.mean(y * y, axis=-1, keepdims=True) + NORM_EPS) * g


def _rmsnorm_kernel(x_ref, g_ref, o_ref):
    o_ref[...] = _rms(x_ref[...], g_ref[...]).astype(o_ref.dtype)


def _rmsnorm(x, g, out_dtype=BF16, tm=1024):
    n, d = x.shape
    return pl.pallas_call(
        _rmsnorm_kernel,
        grid=(n // tm,),
        in_specs=[pl.BlockSpec((tm, d), lambda i: (i, 0)), pl.BlockSpec((1, d), lambda i: (0, 0))],
        out_specs=pl.BlockSpec((tm, d), lambda i: (i, 0)),
        out_shape=jax.ShapeDtypeStruct((n, d), out_dtype),
        compiler_params=_cparams(("parallel",)),
        name="rmsnorm",
    )(x, g.reshape(1, d))


def _col_tile(n, cap=2560):
    best = LANE
    for t in range(LANE, min(n, cap) + 1, LANE):
        if n % t == 0:
            best = t
    return best


def _row_tile(m, cap=1024):
    for t in (1024, 512, 256, 128):
        if t <= cap and m % t == 0:
            return t
    return m


def _mm_kernel(x_ref, w_ref, o_ref, *, act):
    y = _dot(x_ref[...], w_ref[...])
    if act == "relu2":
        y = jnp.square(jnp.maximum(y, 0.0))
    o_ref[...] = y.astype(o_ref.dtype)


def _mm(x, w, out_dtype=F32, act=None, tn_cap=2560):
    m, k = x.shape
    _, n = w.shape
    x = x.astype(BF16)
    w = w.astype(BF16)
    tm = _row_tile(m, 512 if jnp.dtype(out_dtype).itemsize == 4 else 1024)
    tn = _col_tile(n, tn_cap)
    return pl.pallas_call(
        functools.partial(_mm_kernel, act=act),
        grid=(n // tn, m // tm),
        in_specs=[pl.BlockSpec((tm, k), lambda j, i: (i, 0)), pl.BlockSpec((k, tn), lambda j, i: (0, j))],
        out_specs=pl.BlockSpec((tm, tn), lambda j, i: (i, j)),
        out_shape=jax.ShapeDtypeStruct((m, n), out_dtype),
        compiler_params=_cparams(("parallel", "parallel")),
        name="matmul",
    )(x, w)


def _mm_t_kernel(wt_ref, x_ref, o_ref):
    o_ref[...] = _dot(wt_ref[...], x_ref[...], _NT).astype(o_ref.dtype)


def _mm_t(wt, x, out_dtype=BF16):
    n, k = wt.shape
    m, _ = x.shape
    tm = _row_tile(m)
    return pl.pallas_call(
        _mm_t_kernel,
        grid=(m // tm,),
        in_specs=[pl.BlockSpec((n, k), lambda i: (0, 0)), pl.BlockSpec((tm, k), lambda i: (i, 0))],
        out_specs=pl.BlockSpec((n, tm), lambda i: (0, i)),
        out_shape=jax.ShapeDtypeStruct((n, m), out_dtype),
        compiler_params=_cparams(("parallel",)),
        name="matmul_t",
    )(wt.astype(BF16), x.astype(BF16))


def _mm_rope_kernel(x_ref, w_ref, c_ref, s1_ref, s2_ref, o_ref, *, shift):
    y = _dot(x_ref[...], w_ref[...])
    tw = c_ref.shape[-1]
    for g in range(y.shape[1] // LANE):
        yg = y[:, g * LANE:(g + 1) * LANE]
        ts = slice((g * LANE) % tw, (g * LANE) % tw + LANE)
        c, s1, s2 = c_ref[0, :, ts], s1_ref[0, :, ts], s2_ref[0, :, ts]
        out = yg * c + pltpu.roll(yg, shift, 1) * s1 + pltpu.roll(yg, LANE - shift, 1) * s2
        o_ref[:, g * LANE:(g + 1) * LANE] = out.astype(o_ref.dtype)


def _mm_rope(x, w, tabs, shift, out_dtype, tn, seq):
    m, k = x.shape
    _, n = w.shape
    x = x.astype(BF16)
    w = w.astype(BF16)
    tm = _row_tile(min(m, seq), 512)
    tpb = seq // tm
    tw = tabs[0].shape[-1]
    tspec = pl.BlockSpec((1, tm, tw), lambda j, i: (j, i % tpb, 0))
    return pl.pallas_call(
        functools.partial(_mm_rope_kernel, shift=shift),
        grid=(n // tn, m // tm),
        in_specs=[pl.BlockSpec((tm, k), lambda j, i: (i, 0)), pl.BlockSpec((k, tn), lambda j, i: (0, j)),
                  tspec, tspec, tspec],
        out_specs=pl.BlockSpec((tm, tn), lambda j, i: (i, j)),
        out_shape=jax.ShapeDtypeStruct((m, n), out_dtype),
        compiler_params=_cparams(("parallel", "parallel")),
        name="matmul_rope",
    )(x, w, *tabs)


def _rope_tables(seq, head, n_heads, rot, width, scale=1.0):
    half = rot // 2
    inv = ROPE_THETA ** (-np.arange(half, dtype=np.float32) * 2.0 / rot)
    ang = jnp.arange(seq, dtype=F32)[:, None] * jnp.asarray(inv, F32)[None, :]
    cos, sin = jnp.cos(ang), jnp.sin(ang)
    zeros = jnp.zeros((seq, head - rot), F32)
    c_head = jnp.concatenate([cos, cos, zeros + 1.0], axis=1)
    s1_head = jnp.concatenate([jnp.zeros_like(sin), sin, zeros], axis=1)
    s2_head = jnp.concatenate([-sin, jnp.zeros_like(sin), zeros], axis=1)
    rest = width - head * n_heads
    c = jnp.concatenate([c_head] * n_heads + [jnp.ones((seq, rest), F32)], axis=1)
    s1 = jnp.concatenate([s1_head] * n_heads + [jnp.zeros((seq, rest), F32)], axis=1)
    s2 = jnp.concatenate([s2_head] * n_heads + [jnp.zeros((seq, rest), F32)], axis=1)
    return c * scale, s1 * scale, s2 * scale


def _mm_norm_res_kernel(x_ref, w_ref, g_ref, r_ref, g2_ref, o_ref, h_ref):
    y = r_ref[...] + _rms(_dot(x_ref[...], w_ref[...]), g_ref[...])
    o_ref[...] = y
    h_ref[...] = _rms(y, g2_ref[...]).astype(h_ref.dtype)


def _mm_norm_res(x, w, g, res, g_next, tm=512):
    m, k = x.shape
    _, n = w.shape
    x = x.astype(BF16)
    w = w.astype(BF16)
    row = pl.BlockSpec((tm, n), lambda i: (i, 0))
    vec = pl.BlockSpec((1, n), lambda i: (0, 0))
    return pl.pallas_call(
        _mm_norm_res_kernel,
        grid=(m // tm,),
        in_specs=[pl.BlockSpec((tm, k), lambda i: (i, 0)), pl.BlockSpec((k, n), lambda i: (0, 0)), vec, row, vec],
        out_specs=[row, row],
        out_shape=[jax.ShapeDtypeStruct((m, n), F32), jax.ShapeDtypeStruct((m, n), BF16)],
        compiler_params=_cparams(("parallel",)),
        name="matmul_norm_residual",
    )(x, w, g.reshape(1, n), res, g_next.reshape(1, n))


DSA_TQ = 128
DSA_CK = 512


def _dsa_mask_kernel(qc_ref, kc_ref, wt_ref, tri_ref, bias_ref, key_ref, *, seq, topk):
    tq, ck = DSA_TQ, DSA_CK
    qb = pl.program_id(1)
    nch = qb // (ck // tq) + 1
    qpos = qb * tq + lax.broadcasted_iota(jnp.int32, (ck, tq), 1)
    kiota = lax.broadcasted_iota(jnp.int32, (ck, tq), 0)
    qcat = qc_ref[0].reshape(IDX_HEADS * tq, 3 * IDX_HEAD_DIM)
    wrows = [wt_ref[0, h:h + 1, :] for h in range(IDX_HEADS)]

    def score_chunk(c, carry):
        off = pl.multiple_of(c * ck, ck)
        d = _dot(kc_ref[0, pl.ds(off, ck), :], qcat, _NT)
        s = jnp.zeros((ck, tq), F32)
        for h in range(IDX_HEADS):
            s = s + jnp.maximum(d[:, h * tq:(h + 1) * tq], 0.0) * wrows[h]
        s = s + 0.0
        bits = pltpu.bitcast(s, jnp.int32)
        key = bits ^ ((bits >> 31) & jnp.int32(0x7FFFFFFF))
        key_ref[pl.ds(off, ck), :] = jnp.where(off + kiota <= qpos, key, INT32_MIN)
        return carry

    lax.fori_loop(0, nch, score_chunk, 0)

    def count(pred):
        def body(c, acc):
            off = pl.multiple_of(c * ck, ck)
            m = jnp.where(pred(key_ref[pl.ds(off, ck), :]), 1, 0)
            return acc + jnp.sum(m.reshape(ck // SUBLANE, SUBLANE, tq), axis=0)
        acc = lax.fori_loop(0, nch, body, jnp.zeros((SUBLANE, tq), jnp.int32))
        return jnp.sum(acc, axis=0, keepdims=True)

    def bit_body(i, lo):
        cand = lo + (jnp.int32(1) << (31 - i))
        return jnp.where(count(lambda k: k >= cand) >= topk, cand, lo)

    tau = lax.fori_loop(0, 32, bit_body, jnp.full((1, tq), INT32_MIN, jnp.int32))
    n_gt = count(lambda k: k > tau)
    n_ge = count(lambda k: k >= tau)
    need = (topk - n_gt).astype(F32)
    few = qb * tq + lax.broadcasted_iota(jnp.int32, (1, tq), 1) + 1 <= topk
    no_cut = jnp.min(jnp.where(jnp.logical_or(n_ge == topk, few), 1, 0)) == 1

    def emit(c, take):
        off = pl.multiple_of(c * ck, ck)
        bias_ref[0, pl.ds(off, ck), :] = jnp.where(take, 0.0, -jnp.inf).astype(bias_ref.dtype)

    @pl.when(no_cut)
    def _():
        def chunk(c, carry):
            off = pl.multiple_of(c * ck, ck)
            k = key_ref[pl.ds(off, ck), :]
            emit(c, jnp.logical_and(k >= tau, off + kiota <= qpos))
            return carry
        lax.fori_loop(0, nch, chunk, 0)

    @pl.when(jnp.logical_not(no_cut))
    def _():
        def chunk(c, run):
            off = pl.multiple_of(c * ck, ck)
            k = key_ref[pl.ds(off, ck), :]
            eq = jnp.logical_and(k == tau, off + kiota <= qpos)
            eqf = jnp.where(eq, 1.0, 0.0)
            before = _dot(tri_ref[...], eqf.astype(BF16)) + run
            emit(c, jnp.logical_or(k > tau, jnp.logical_and(eq, before < need)))
            return run + jnp.sum(eqf, axis=0, keepdims=True)
        lax.fori_loop(0, nch, chunk, jnp.zeros((1, tq), F32))

    def fill_chunk(c, carry):
        off = pl.multiple_of(c * ck, ck)
        bias_ref[0, pl.ds(off, ck), :] = jnp.full((ck, tq), -jnp.inf, bias_ref.dtype)
        return carry

    lax.fori_loop(nch, seq // ck, fill_chunk, 0)


def _dsa_mask(idx, topk):
    bsz, seq, _ = idx.shape
    tq, ck = DSA_TQ, DSA_CK
    wi0 = IDX_Q_WIDTH + IDX_HEAD_DIM
    q_hi, q_lo = _split2(idx[:, :, :IDX_Q_WIDTH].reshape(bsz, seq, IDX_HEADS, IDX_HEAD_DIM))
    qcat = jnp.concatenate([q_hi, q_hi, q_lo], axis=-1).swapaxes(1, 2)
    k_hi, k_lo = _split2(idx[:, :, IDX_Q_WIDTH:wi0])
    kcat = jnp.concatenate([k_hi, k_lo, k_hi], axis=-1)
    wt = (idx[:, :, wi0:wi0 + IDX_HEADS] * IDX_HEAD_DIM ** -0.5).swapaxes(1, 2)
    wt = jnp.pad(wt, ((0, 0), (0, SUBLANE - IDX_HEADS), (0, 0)))
    tri = jnp.asarray(np.tril(np.ones((ck, ck), np.float32), -1), BF16)
    return pl.pallas_call(
        functools.partial(_dsa_mask_kernel, seq=seq, topk=topk),
        grid=(bsz, seq // tq),
        in_specs=[pl.BlockSpec((1, IDX_HEADS, tq, 3 * IDX_HEAD_DIM), lambda b, q: (b, 0, q, 0)),
                  pl.BlockSpec((1, seq, 3 * IDX_HEAD_DIM), lambda b, q: (b, 0, 0)),
                  pl.BlockSpec((1, SUBLANE, tq), lambda b, q: (b, 0, q)),
                  pl.BlockSpec((ck, ck), lambda b, q: (0, 0))],
        out_specs=pl.BlockSpec((1, seq, tq), lambda b, q: (b, 0, q)),
        out_shape=jax.ShapeDtypeStruct((bsz, seq, seq), BF16),
        scratch_shapes=[pltpu.VMEM((seq, tq), jnp.int32)],
        compiler_params=_cparams(("parallel", "parallel")),
        name="dsa_topk_mask",
    )(qcat, kcat, wt, tri)


ATT_TQ = 256
ATT_TK = 512


def _dsa_attn_kernel(q_ref, k_ref, vt_ref, b_ref, o_ref, m_sc, acc_sc, bias_sc):
    tq, tk, hd = ATT_TQ, ATT_TK, ATT_HEAD_DIM
    he = hd + SUBLANE
    qb = pl.program_id(1)
    kb = pl.program_id(2)

    @pl.when(kb == 0)
    def _():
        m_sc[...] = jnp.full(m_sc.shape, -jnp.inf, F32)
        acc_sc[...] = jnp.zeros(acc_sc.shape, F32)

    @pl.when(kb * tk <= qb * tq + (tq - 1))
    def _():
        bias_sc[...] = b_ref[0].astype(F32)
        ones = jnp.ones((SUBLANE, tk), BF16)
        hs = range(ATT_HEADS)
        cs = [slice(h * hd, (h + 1) * hd) for h in hs]
        s = [_dot(k_ref[0, :, cs[h]], q_ref[0, :, cs[h]], _NT) + bias_sc[...] for h in hs]
        m_prev = [m_sc[h] for h in hs]
        m_new = [jnp.maximum(m_prev[h], jnp.max(s[h], axis=0, keepdims=True)) for h in hs]
        m_use = [jnp.where(m_new[h] == -jnp.inf, 0.0, m_new[h]) for h in hs]
        p = [jnp.exp2(s[h] - m_use[h]).astype(BF16) for h in hs]
        alpha = [jnp.exp2(m_prev[h] - m_use[h]) for h in hs]
        pv = [_dot(jnp.concatenate([vt_ref[cs[h], :], ones], axis=0), p[h]) for h in hs]
        for h in hs:
            rs = slice(h * he, (h + 1) * he)
            acc_sc[rs, :] = alpha[h] * acc_sc[rs, :] + pv[h]
            m_sc[h] = m_new[h]

    @pl.when(kb == pl.num_programs(2) - 1)
    def _():
        for h in range(ATT_HEADS):
            o_t = acc_sc[h * he:h * he + hd, :] / acc_sc[h * he + hd:h * he + hd + 1, :]
            o_ref[0, :, h * hd:(h + 1) * hd] = jnp.transpose(o_t).astype(o_ref.dtype)


def _dsa_attend(qk, vt, bias_t, out_dtype=BF16):
    bsz, seq, _ = qk.shape
    width = vt.shape[0]
    tq, tk = ATT_TQ, ATT_TK
    nkb = seq // tk

    def kb_of(i, j):
        return jnp.minimum(j, (i * tq + tq - 1) // tk)

    return pl.pallas_call(
        _dsa_attn_kernel,
        grid=(bsz, seq // tq, nkb),
        in_specs=[pl.BlockSpec((1, tq, width), lambda b, i, j: (b, i, 0)),
                  pl.BlockSpec((1, tk, width), lambda b, i, j: (b, kb_of(i, j), 1)),
                  pl.BlockSpec((width, tk), lambda b, i, j: (0, b * nkb + kb_of(i, j))),
                  pl.BlockSpec((1, tk, tq), lambda b, i, j: (b, kb_of(i, j), i))],
        out_specs=pl.BlockSpec((1, tq, width), lambda b, i, j: (b, i, 0)),
        out_shape=jax.ShapeDtypeStruct((bsz, seq, width), out_dtype),
        scratch_shapes=[pltpu.VMEM((ATT_HEADS, 1, tq), F32),
                        pltpu.VMEM((ATT_HEADS * (ATT_HEAD_DIM + SUBLANE), tq), F32),
                        pltpu.VMEM((tk, tq), F32)],
        compiler_params=_cparams(("parallel", "parallel", "arbitrary")),
        name="dsa_attention",
    )(qk, qk, vt, bias_t)


RW_L = 64
RW_SEG = 256


def _head_sums(x, bd_ref):
    rows = x.shape[0]
    hi, lo = _split2(x)
    st = jnp.concatenate([hi, lo], axis=0)
    y = jnp.concatenate([_dot(st[:, g * RW_SEG:(g + 1) * RW_SEG], bd_ref[...])
                         for g in range(RWKV_WIDTH // RW_SEG)], axis=1)
    return y[:rows] + y[rows:]


def _rwkv_kernel(p_ref, halo_ref, mu_ref, w0_ref, w2_ref, a0_ref, a2_ref, g2_ref, kk_ref, ka_ref, rk_ref,
                 lng_ref, lnb_ref, tril_ref, bd_ref, o_ref, s_ref):
    L, hd, W = RW_L, RWKV_HEAD, RWKV_WIDTH
    c = pl.program_id(1)

    @pl.when(c == 0)
    def _():
        s_ref[...] = jnp.zeros(s_ref.shape, F32)

    p = p_ref[0]
    first = halo_ref[0, SUBLANE - 1:SUBLANE, :] * jnp.where(c > 0, 1.0, 0.0)
    row = lax.broadcasted_iota(jnp.int32, p.shape, 0)
    prev = jnp.where(row == 0, first, pltpu.roll(p, 1, 0))
    p = p + (prev - p) * mu_ref[...]
    r = p[:, 0:W]
    k = p[:, W:2 * W]
    v = p[:, 2 * W:3 * W]
    o_lora = 3 * W
    wl = p[:, o_lora:o_lora + DECAY_LORA]
    al = p[:, o_lora + DECAY_LORA:o_lora + DECAY_LORA + AAA_LORA]
    gl = p[:, o_lora + DECAY_LORA + AAA_LORA:]
    w = -_softplus(-(w0_ref[...] + _dot(jnp.tanh(wl).astype(BF16), w2_ref[...]))) - 0.5
    lw = -jnp.exp(w)
    a = _sigmoid(a0_ref[...] + _dot(al.astype(BF16), a2_ref[...]))
    gate = _dot(_sigmoid(gl).astype(BF16), g2_ref[...])
    khat = k * kk_ref[...]
    kk = khat / jnp.maximum(jnp.sqrt(_head_sums(khat * khat, bd_ref)), 1e-12)
    k = k * (1.0 + (a - 1.0) * ka_ref[...])
    av = -kk
    bv = kk * a

    tril = tril_ref[...]
    l1, l2, l3 = _split3(lw)
    cum = _dot(tril, l1) + (_dot(tril, l2) + _dot(tril, l3))
    tot = cum[L - 1:L, :]
    e_neg = jnp.exp(-cum)
    e_end = jnp.exp(tot - cum)
    w_tot = jnp.exp(tot)
    rt = r * jnp.exp(cum)
    at = av * jnp.exp(cum - lw)
    bt = bv * e_neg
    kt = k * e_neg
    bh = bv * e_end
    kh = k * e_end

    rw = lax.broadcasted_iota(jnp.int32, (2 * L, L), 0)
    cl = lax.broadcasted_iota(jnp.int32, (2 * L, L), 1)
    keep = cl < jnp.where(rw < L, rw, rw - (L - 1))
    eye = lax.broadcasted_iota(jnp.int32, (L, L), 0) == lax.broadcasted_iota(jnp.int32, (L, L), 1)

    hs = range(RWKV_HEADS)
    cs = [slice(h * hd, (h + 1) * hd) for h in hs]
    lhs = [jnp.concatenate([at[:, s], rt[:, s]], axis=0) for s in cs]
    a_b = [jnp.where(keep, _dot_b(lhs[h], bt[:, cs[h]], _NT), 0.0) for h in hs]
    a_k = [jnp.where(keep, _dot_b(lhs[h], kt[:, cs[h]], _NT), 0.0) for h in hs]
    n = [a_b[h][:L] for h in hs]
    ident = jnp.where(eye, 1.0, 0.0)
    t_inv = [ident + n[h] for h in hs]
    for _ in range(int(math.log2(L)) - 1):
        n = [_dot_b(n[h], n[h]) for h in hs]
        t_inv = [t_inv[h] + _dot_b(t_inv[h], n[h]) for h in hs]
    akv = [_dot_b(a_k[h], v[:, cs[h]]) for h in hs]
    p12 = [_dot_b(t_inv[h], jnp.concatenate([at[:, cs[h]], akv[h][:L]], axis=1)) for h in hs]
    q12 = [_dot_b(a_b[h][L:], p12[h]) + jnp.concatenate([rt[:, cs[h]], akv[h][L:]], axis=1) for h in hs]
    s = [s_ref[h] for h in hs]
    y = jnp.concatenate([_dot_b(q12[h][:, :hd], s[h], _NT) + q12[h][:, hd:] for h in hs], axis=1)
    g = [_dot_b(p12[h], bh[:, cs[h]], _TN) for h in hs]
    vk = [_dot_b(v[:, cs[h]], kh[:, cs[h]], _TN) for h in hs]
    s_new = [s[h] * w_tot[:, cs[h]] + _dot_b(s[h], g[h][:hd]) + g[h][hd:] + vk[h] for h in hs]
    for h in hs:
        s_ref[h] = s_new[h]

    inv_hd = 1.0 / hd
    mean = _head_sums(y, bd_ref) * inv_hd
    yc = y - mean
    var = _head_sums(yc * yc, bd_ref) * inv_hd
    yn = yc * lax.rsqrt(var + RWKV_GN_EPS) * lng_ref[...] + lnb_ref[...]
    bonus = _head_sums(r * k * rk_ref[...], bd_ref) * v
    o_ref[0] = ((yn + bonus) * gate).astype(o_ref.dtype)


def _rwkv_branch(pb, mu, w0, w2, a0, a2, g2, k_k, k_a, r_k, lnx_g, lnx_b, out_dtype=BF16):
    bsz, seq, cols = pb.shape
    L, W = RW_L, RWKV_WIDTH
    tril = jnp.asarray(np.tril(np.ones((L, L), np.float32)), BF16)
    seg = np.arange(RW_SEG) // RWKV_HEAD
    bd = jnp.asarray((seg[:, None] == seg[None, :]).astype(np.float32), BF16)

    def vec(a):
        return a.reshape(1, -1).astype(F32)

    def full(shape):
        return pl.BlockSpec(shape, lambda b, c: (0,) * len(shape))

    return pl.pallas_call(
        _rwkv_kernel,
        grid=(bsz, seq // L),
        in_specs=[pl.BlockSpec((1, L, cols), lambda b, c: (b, c, 0)),
                  pl.BlockSpec((1, SUBLANE, cols), lambda b, c: (b, jnp.maximum(c * (L // SUBLANE) - 1, 0), 0)),
                  full((1, cols)), full((1, W)), full((DECAY_LORA, W)), full((1, W)), full((AAA_LORA, W)),
                  full((GATE_LORA, W)), full((1, W)), full((1, W)), full((1, W)), full((1, W)), full((1, W)),
                  full((L, L)), full((RW_SEG, RW_SEG))],
        out_specs=pl.BlockSpec((1, L, W), lambda b, c: (b, c, 0)),
        out_shape=jax.ShapeDtypeStruct((bsz, seq, W), out_dtype),
        scratch_shapes=[pltpu.VMEM((RWKV_HEADS, RWKV_HEAD, RWKV_HEAD), F32)],
        compiler_params=_cparams(("parallel", "arbitrary")),
        name="rwkv7_time_mix",
    )(pb, pb, vec(mu), vec(w0), w2.astype(BF16), vec(a0), a2.astype(BF16), g2.astype(BF16), vec(k_k), vec(k_a),
      vec(r_k), vec(lnx_g), vec(lnx_b), tril, bd)


def _mamba_kernel(p_ref, halo_ref, cw_ref, cb_ref, dtb_ref, alog_ref, dexp_ref, ng_ref, e_ref, triu_ref,
                  o_ref, h_ref, y_sc):
    L, hd, ns = SSM_CHUNK, SSM_HEAD_DIM, SSM_STATE
    hg = SSM_HEADS // SSM_GROUPS
    gw = hg * hd
    c = pl.program_id(1)

    @pl.when(c == 0)
    def _():
        h_ref[...] = jnp.zeros(h_ref.shape, F32)

    x0 = SSM_INNER
    xr = p_ref[0, :, x0:x0 + SSM_CONV_CH]
    halo = halo_ref[0, :, x0:x0 + SSM_CONV_CH] * jnp.where(c > 0, 1.0, 0.0)
    row8 = lax.broadcasted_iota(jnp.int32, halo.shape, 0)
    acc = xr * cw_ref[SSM_CONV - 1:SSM_CONV, :] + cb_ref[...]
    for s in range(1, SSM_CONV):
        rolled = pltpu.roll(xr, s, 0)
        top = jnp.where(row8 < s, pltpu.roll(halo, s, 0), rolled[0:SUBLANE])
        shifted = jnp.concatenate([top, rolled[SUBLANE:]], axis=0)
        acc = acc + shifted * cw_ref[SSM_CONV - 1 - s:SSM_CONV - s, :]
    xbc = acc * _sigmoid(acc)
    xs = xbc[:, :SSM_INNER]
    bm = xbc[:, SSM_INNER:SSM_INNER + SSM_GROUPS * ns]
    cm = xbc[:, SSM_INNER + SSM_GROUPS * ns:]

    dt = _softplus(p_ref[0, :, x0 + SSM_CONV_CH:x0 + SSM_CONV_CH + SSM_HEADS] + dtb_ref[...])
    a = dt * (-jnp.exp(alog_ref[...]))
    triu = triu_ref[...]
    a_cum_t = _dot_exact_rhs(_split3(a), triu, _TN)
    a_cum_e = _dot_exact_rhs(_split3(a_cum_t), e_ref[...], _TN)
    dt_e = _dot_exact_rhs(_split3(dt), e_ref[...])
    xdt = xs * dt_e
    tot = a_cum_e[L - 1:L, :]
    e_in = jnp.exp(a_cum_e)
    xw = xdt * jnp.exp(tot - a_cum_e)
    e_tot = jnp.exp(tot)
    lower = lax.broadcasted_iota(jnp.int32, (L, L), 1) <= lax.broadcasted_iota(jnp.int32, (L, L), 0)

    for g in range(SSM_GROUPS):
        gs = slice(g * gw, (g + 1) * gw)
        bg = bm[:, g * ns:(g + 1) * ns]
        cg = cm[:, g * ns:(g + 1) * ns]
        cb = _dot_b(cg, bg, _NT)
        ht = h_ref[g]
        y_off = _dot_b(cg, ht) * e_in[:, gs]
        for j in range(hg):
            h = g * hg + j
            hsl = slice(h * hd, (h + 1) * hd)
            seg = a_cum_e[:, h * hd:h * hd + 1] - a_cum_t[h:h + 1, :]
            m = cb * jnp.exp(jnp.where(lower, seg, -jnp.inf))
            y_sc[:, hsl] = _dot_b(m, xdt[:, hsl]) + y_off[:, j * hd:(j + 1) * hd]
        h_ref[g] = ht * e_tot[:, gs] + _dot_b(bg, xw[:, gs], _TN)

    z = p_ref[0, :, 0:SSM_INNER]
    y = (y_sc[...] + xs * dexp_ref[...]) * (z * _sigmoid(z))
    for g in range(SSM_GROUPS):
        gs = slice(g * gw, (g + 1) * gw)
        o_ref[0, :, gs] = _rms(y[:, gs], ng_ref[:, gs]).astype(o_ref.dtype)


def _mamba_branch(pc, conv_w, conv_b, dt_bias, a_log, d_skip, norm_g, out_dtype=BF16):
    bsz, seq, cols = pc.shape
    L = SSM_CHUNK
    triu = jnp.asarray(np.triu(np.ones((L, L), np.float32)), BF16)
    expand = jnp.asarray(np.repeat(np.eye(SSM_HEADS, dtype=np.float32), SSM_HEAD_DIM, axis=1), BF16)

    def full(shape):
        return pl.BlockSpec(shape, lambda b, c: (0,) * len(shape))

    return pl.pallas_call(
        _mamba_kernel,
        grid=(bsz, seq // L),
        in_specs=[pl.BlockSpec((1, L, cols), lambda b, c: (b, c, 0)),
                  pl.BlockSpec((1, SUBLANE, cols), lambda b, c: (b, jnp.maximum(c * (L // SUBLANE) - 1, 0), 0)),
                  full((SSM_CONV, SSM_CONV_CH)), full((1, SSM_CONV_CH)), full((1, SSM_HEADS)), full((1, SSM_HEADS)),
                  full((1, SSM_INNER)), full((1, SSM_INNER)), full((SSM_HEADS, SSM_INNER)), full((L, L))],
        out_specs=pl.BlockSpec((1, L, SSM_INNER), lambda b, c: (b, c, 0)),
        out_shape=jax.ShapeDtypeStruct((bsz, seq, SSM_INNER), out_dtype),
        scratch_shapes=[pltpu.VMEM((SSM_GROUPS, SSM_STATE, SSM_INNER // SSM_GROUPS), F32),
                        pltpu.VMEM((L, SSM_INNER), F32)],
        compiler_params=_cparams(("parallel", "arbitrary")),
        name="mamba2_mixer",
    )(pc, pc, conv_w, conv_b.reshape(1, -1), dt_bias.reshape(1, -1), a_log.reshape(1, -1),
      jnp.repeat(d_skip, SSM_HEAD_DIM).reshape(1, -1), norm_g.reshape(1, -1), expand, triu)


def _merge_kernel(oa_ref, ob_ref, oc_ref, pg_ref, x_ref, wa_ref, wb_ref, wc_ref, wo_ref, g1_ref, g2_ref,
                  xo_ref, hf_ref):
    d = D_MODEL
    merged = (_sigmoid(pg_ref[:, 0:d].astype(F32)) * _dot(oa_ref[...], wa_ref[...])
              + _sigmoid(pg_ref[:, d:2 * d].astype(F32)) * _dot(ob_ref[...], wb_ref[...])
              + _sigmoid(pg_ref[:, 2 * d:3 * d].astype(F32)) * _dot(oc_ref[...], wc_ref[...]))
    x = x_ref[...] + _rms(_dot(merged.astype(BF16), wo_ref[...]), g1_ref[...])
    xo_ref[...] = x
    hf_ref[...] = _rms(x, g2_ref[...]).astype(hf_ref.dtype)


def _merge_out(o_a, o_b, o_c, pg, x, w_o_att, w_o_rwkv, w_o_ssm, w_out, post_mix_g, pre_ffn_g, tm=256):
    n, d = x.shape

    def row(width):
        return pl.BlockSpec((tm, width), lambda i: (i, 0))

    def full(shape):
        return pl.BlockSpec(shape, lambda i: (0, 0))

    return pl.pallas_call(
        _merge_kernel,
        grid=(n // tm,),
        in_specs=[row(ATT_WIDTH), row(RWKV_WIDTH), row(SSM_INNER), row(GATE_COLS), row(d),
                  full((ATT_WIDTH, d)), full((RWKV_WIDTH, d)), full((SSM_INNER, d)), full((d, d)),
                  full((1, d)), full((1, d))],
        out_specs=[row(d), row(d)],
        out_shape=[jax.ShapeDtypeStruct((n, d), F32), jax.ShapeDtypeStruct((n, d), BF16)],
        compiler_params=_cparams(("parallel",)),
        name="merge_out",
    )(o_a, o_b, o_c, pg, x, w_o_att.astype(BF16), w_o_rwkv.astype(BF16), w_o_ssm.astype(BF16),
      w_out.astype(BF16), post_mix_g.reshape(1, d), pre_ffn_g.reshape(1, d))


def _pad_cols(w, width):
    return jnp.pad(w, ((0, 0), (0, width - w.shape[1])))


def _layer(x, h, bsz, seq, g_next, w_in, w_o_att, rwkv_mu, rwkv_w0, rwkv_w2, rwkv_a0, rwkv_a2, rwkv_g2,
           rwkv_k_k, rwkv_k_a, rwkv_r_k, rwkv_lnx_g, rwkv_lnx_b, w_o_rwkv, ssm_conv_w, ssm_conv_b,
           ssm_dt_bias, ssm_a_log, ssm_d, ssm_norm_g, w_o_ssm, w_out, post_mix_g, pre_ffn_g,
           w_ff1, w_ff2, post_ffn_g):
    n = bsz * seq
    topk = min(TOPK_MAX, seq // 4)
    wa, wb, wc, wg = jnp.split(w_in, np.cumsum([A_COLS, B_COLS, C_COLS]).tolist(), axis=1)

    q_scale = ATT_HEAD_DIM ** -0.5 * math.log2(math.e)
    tq_ = _rope_tables(seq, ATT_HEAD_DIM, 1, ATT_ROT, LANE, q_scale)
    tk_ = _rope_tables(seq, ATT_HEAD_DIM, 1, ATT_ROT, LANE)
    qk = _mm_rope(h, wa[:, :2 * ATT_WIDTH], tuple(jnp.stack([a, b]) for a, b in zip(tq_, tk_)), ATT_ROT // 2,
                  BF16, ATT_WIDTH, seq).reshape(bsz, seq, 2 * ATT_WIDTH)
    vt = _mm_t(wa[:, 2 * ATT_WIDTH:3 * ATT_WIDTH].T, h)
    ti = list(_rope_tables(seq, IDX_HEAD_DIM, IDX_HEADS + 1, IDX_ROT, IDX_COLS))
    wi0 = IDX_Q_WIDTH + IDX_HEAD_DIM
    ti[0] = ti[0].at[:, wi0:wi0 + IDX_HEADS].set(IDX_HEADS ** -0.5)
    idx = _mm_rope(h, _pad_cols(wa[:, 3 * ATT_WIDTH:], IDX_COLS), tuple(t[None] for t in ti), IDX_ROT // 2,
                   F32, IDX_COLS, seq).reshape(bsz, seq, IDX_COLS)
    mask = _dsa_mask(idx, topk)
    o_a = _dsa_attend(qk, vt, mask).reshape(n, ATT_WIDTH)

    pb = _mm(h, wb, F32).reshape(bsz, seq, B_COLS)
    o_b = _rwkv_branch(pb, rwkv_mu, rwkv_w0, rwkv_w2, rwkv_a0, rwkv_a2, rwkv_g2, rwkv_k_k, rwkv_k_a, rwkv_r_k,
                       rwkv_lnx_g, rwkv_lnx_b).reshape(n, RWKV_WIDTH)
    pc = _mm(h, _pad_cols(wc, C_COLS_PAD), F32).reshape(bsz, seq, C_COLS_PAD)
    o_c = _mamba_branch(pc, ssm_conv_w, ssm_conv_b, ssm_dt_bias, ssm_a_log, ssm_d, ssm_norm_g).reshape(n, SSM_INNER)

    pg = _mm(h, wg, BF16)
    x, hf = _merge_out(o_a, o_b, o_c, pg, x, w_o_att, w_o_rwkv, w_o_ssm, w_out, post_mix_g, pre_ffn_g)
    f = _mm(hf, w_ff1, out_dtype=BF16, act="relu2")
    return _mm_norm_res(f, w_ff2, post_ffn_g, x, g_next)


def kernel(x, pre_mix_g, w_in, w_o_att, rwkv_mu, rwkv_w0, rwkv_w2, rwkv_a0, rwkv_a2, rwkv_g2, rwkv_k_k,
           rwkv_k_a, rwkv_r_k, rwkv_lnx_g, rwkv_lnx_b, w_o_rwkv, ssm_conv_w, ssm_conv_b, ssm_dt_bias,
           ssm_a_log, ssm_d, ssm_norm_g, w_o_ssm, w_out, post_mix_g, pre_ffn_g, w_ff1, w_ff2, post_ffn_g):
    bsz, seq, d = x.shape
    depth = pre_mix_g.shape[0]
    params = (w_in, w_o_att, rwkv_mu, rwkv_w0, rwkv_w2, rwkv_a0, rwkv_a2, rwkv_g2, rwkv_k_k,
              rwkv_k_a, rwkv_r_k, rwkv_lnx_g, rwkv_lnx_b, w_o_rwkv, ssm_conv_w, ssm_conv_b, ssm_dt_bias,
              ssm_a_log, ssm_d, ssm_norm_g, w_o_ssm, w_out, post_mix_g, pre_ffn_g, w_ff1, w_ff2, post_ffn_g)
    y = x.reshape(bsz * seq, d)
    h = _rmsnorm(y, pre_mix_g[0])
    for i in range(depth):
        g_next = pre_mix_g[min(i + 1, depth - 1)]
        y, h = _layer(y, h, bsz, seq, g_next, *[p[i] for p in params])
    return y.reshape(bsz, seq, d)
```

```python
import functools
import math

import jax
import jax.numpy as jnp
import numpy as np
from jax import lax
from jax.experimental import pallas as pl
from jax.experimental.pallas import tpu as pltpu

F32 = jnp.float32
BF16 = jnp.bfloat16

D_MODEL = 1024
ATT_HEADS = 8
ATT_HEAD_DIM = 128
ATT_WIDTH = ATT_HEADS * ATT_HEAD_DIM
IDX_HEADS = 4
IDX_HEAD_DIM = 64
IDX_Q_WIDTH = IDX_HEADS * IDX_HEAD_DIM
TOPK_MAX = 256
ROPE_THETA = 500000.0
ATT_ROT = ATT_HEAD_DIM // 4
IDX_ROT = IDX_HEAD_DIM // 4
RWKV_HEAD = 64
RWKV_WIDTH = D_MODEL
RWKV_HEADS = RWKV_WIDTH // RWKV_HEAD
DECAY_LORA = 64
AAA_LORA = 64
GATE_LORA = 128
RWKV_GN_EPS = 64e-5
SSM_INNER = 2 * D_MODEL
SSM_HEAD_DIM = 64
SSM_HEADS = SSM_INNER // SSM_HEAD_DIM
SSM_GROUPS = 2
SSM_STATE = 128
SSM_CONV = 4
SSM_CHUNK = 128
SSM_CONV_CH = SSM_INNER + 2 * SSM_GROUPS * SSM_STATE
D_FF = 4 * D_MODEL
N_BRANCH = 3
NORM_EPS = 1e-6

A_SIZES = [ATT_WIDTH] * 3 + [IDX_Q_WIDTH, IDX_HEAD_DIM, IDX_HEADS]
A_COLS = sum(A_SIZES)
B_COLS = 3 * RWKV_WIDTH + DECAY_LORA + AAA_LORA + GATE_LORA
C_COLS = SSM_INNER + SSM_CONV_CH + SSM_HEADS
GATE_COLS = N_BRANCH * D_MODEL

V7X_VMEM_BYTES = 64 * 1024 * 1024
VMEM_LIMIT = 56 * 1024 * 1024
LANE = 128
SUBLANE = 8

IDX_COLS = 3 * LANE
C_COLS_PAD = 38 * LANE

INT32_MIN = np.int32(-(2 ** 31))


def _cparams(sem):
    return pltpu.CompilerParams(dimension_semantics=sem, vmem_limit_bytes=VMEM_LIMIT)


def _split2(a):
    hi = a.astype(BF16)
    lo = (a - hi.astype(F32)).astype(BF16)
    return hi, lo


def _split3(a):
    h1 = a.astype(BF16)
    r1 = a - h1.astype(F32)
    h2 = r1.astype(BF16)
    h3 = (r1 - h2.astype(F32)).astype(BF16)
    return h1, h2, h3


_NN = (((1,), (0,)), ((), ()))
_NT = (((1,), (1,)), ((), ()))
_TN = (((0,), (0,)), ((), ()))


def _dot(a, b, dims=_NN):
    return lax.dot_general(a, b, dims, preferred_element_type=F32)


def _dot_b(a, b, dims=_NN):
    return _dot(a.astype(BF16), b.astype(BF16), dims)


def _dot_exact_rhs(parts, mat, dims=_NN):
    out = _dot(parts[0], mat, dims)
    for p in parts[1:]:
        out = out + _dot(p, mat, dims)
    return out


def _softplus(x):
    return jnp.maximum(x, 0.0) + jnp.log(1.0 + jnp.exp(-jnp.abs(x)))


def _sigmoid(x):
    return 1.0 / (1.0 + jnp.exp(-x))


def _rms(y, g):
    return y * lax.rsqrt(jnp.mean(y * y, axis=-1, keepdims=True) + NORM_EPS) * g


def _rmsnorm_kernel(x_ref, g_ref, o_ref):
    o_ref[...] = _rms(x_ref[...], g_ref[...]).astype(o_ref.dtype)


def _rmsnorm(x, g, out_dtype=BF16, tm=1024):
    n, d = x.shape
    return pl.pallas_call(
        _rmsnorm_kernel,
        grid=(n // tm,),
        in_specs=[pl.BlockSpec((tm, d), lambda i: (i, 0)), pl.BlockSpec((1, d), lambda i: (0, 0))],
        out_specs=pl.BlockSpec((tm, d), lambda i: (i, 0)),
        out_shape=jax.ShapeDtypeStruct((n, d), out_dtype),
        compiler_params=_cparams(("parallel",)),
        name="rmsnorm",
    )(x, g.reshape(1, d))


def _col_tile(n, cap=2560):
    best = LANE
    for t in range(LANE, min(n, cap) + 1, LANE):
        if n % t == 0:
            best = t
    return best


def _row_tile(m, cap=1024):
    for t in (1024, 512, 256, 128):
        if t <= cap and m % t == 0:
            return t
    return m


def _mm_kernel(x_ref, w_ref, o_ref, *, act):
    y = _dot(x_ref[...], w_ref[...])
    if act == "relu2":
        y = jnp.square(jnp.maximum(y, 0.0))
    o_ref[...] = y.astype(o_ref.dtype)


def _mm(x, w, out_dtype=F32, act=None, tn_cap=2560):
    m, k = x.shape
    _, n = w.shape
    x = x.astype(BF16)
    w = w.astype(BF16)
    tm = _row_tile(m, 512 if jnp.dtype(out_dtype).itemsize == 4 else 1024)
    tn = _col_tile(n, tn_cap)
    return pl.pallas_call(
        functools.partial(_mm_kernel, act=act),
        grid=(n // tn, m // tm),
        in_specs=[pl.BlockSpec((tm, k), lambda j, i: (i, 0)), pl.BlockSpec((k, tn), lambda j, i: (0, j))],
        out_specs=pl.BlockSpec((tm, tn), lambda j, i: (i, j)),
        out_shape=jax.ShapeDtypeStruct((m, n), out_dtype),
        compiler_params=_cparams(("parallel", "parallel")),
        name="matmul",
    )(x, w)


def _mm_t_kernel(wt_ref, x_ref, o_ref):
    o_ref[...] = _dot(wt_ref[...], x_ref[...], _NT).astype(o_ref.dtype)


def _mm_t(wt, x, out_dtype=BF16):
    n, k = wt.shape
    m, _ = x.shape
    tm = _row_tile(m)
    return pl.pallas_call(
        _mm_t_kernel,
        grid=(m // tm,),
        in_specs=[pl.BlockSpec((n, k), lambda i: (0, 0)), pl.BlockSpec((tm, k), lambda i: (i, 0))],
        out_specs=pl.BlockSpec((n, tm), lambda i: (0, i)),
        out_shape=jax.ShapeDtypeStruct((n, m), out_dtype),
        compiler_params=_cparams(("parallel",)),
        name="matmul_t",
    )(wt.astype(BF16), x.astype(BF16))


def _mm_rope_kernel(x_ref, w_ref, c_ref, s1_ref, s2_ref, o_ref, *, shift):
    y = _dot(x_ref[...], w_ref[...])
    tw = c_ref.shape[-1]
    for g in range(y.shape[1] // LANE):
        yg = y[:, g * LANE:(g + 1) * LANE]
        ts = slice((g * LANE) % tw, (g * LANE) % tw + LANE)
        c, s1, s2 = c_ref[0, :, ts], s1_ref[0, :, ts], s2_ref[0, :, ts]
        out = yg * c + pltpu.roll(yg, shift, 1) * s1 + pltpu.roll(yg, LANE - shift, 1) * s2
        o_ref[:, g * LANE:(g + 1) * LANE] = out.astype(o_ref.dtype)


def _mm_rope(x, w, tabs, shift, out_dtype, tn, seq):
    m, k = x.shape
    _, n = w.shape
    x = x.astype(BF16)
    w = w.astype(BF16)
    tm = _row_tile(min(m, seq), 512)
    tpb = seq // tm
    tw = tabs[0].shape[-1]
    tspec = pl.BlockSpec((1, tm, tw), lambda j, i: (j, i % tpb, 0))
    return pl.pallas_call(
        functools.partial(_mm_rope_kernel, shift=shift),
        grid=(n // tn, m // tm),
        in_specs=[pl.BlockSpec((tm, k), lambda j, i: (i, 0)), pl.BlockSpec((k, tn), lambda j, i: (0, j)),
                  tspec, tspec, tspec],
        out_specs=pl.BlockSpec((tm, tn), lambda j, i: (i, j)),
        out_shape=jax.ShapeDtypeStruct((m, n), out_dtype),
        compiler_params=_cparams(("parallel", "parallel")),
        name="matmul_rope",
    )(x, w, *tabs)


def _rope_tables(seq, head, n_heads, rot, width, scale=1.0):
    half = rot // 2
    inv = ROPE_THETA ** (-np.arange(half, dtype=np.float32) * 2.0 / rot)
    ang = jnp.arange(seq, dtype=F32)[:, None] * jnp.asarray(inv, F32)[None, :]
    cos, sin = jnp.cos(ang), jnp.sin(ang)
    zeros = jnp.zeros((seq, head - rot), F32)
    c_head = jnp.concatenate([cos, cos, zeros + 1.0], axis=1)
    s1_head = jnp.concatenate([jnp.zeros_like(sin), sin, zeros], axis=1)
    s2_head = jnp.concatenate([-sin, jnp.zeros_like(sin), zeros], axis=1)
    rest = width - head * n_heads
    c = jnp.concatenate([c_head] * n_heads + [jnp.ones((seq, rest), F32)], axis=1)
    s1 = jnp.concatenate([s1_head] * n_heads + [jnp.zeros((seq, rest), F32)], axis=1)
    s2 = jnp.concatenate([s2_head] * n_heads + [jnp.zeros((seq, rest), F32)], axis=1)
    return c * scale, s1 * scale, s2 * scale


def _mm_norm_res_kernel(x_ref, w_ref, g_ref, r_ref, g2_ref, o_ref, h_ref):
    y = r_ref[...] + _rms(_dot(x_ref[...], w_ref[...]), g_ref[...])
    o_ref[...] = y
    h_ref[...] = _rms(y, g2_ref[...]).astype(h_ref.dtype)


def _mm_norm_res(x, w, g, res, g_next, tm=512):
    m, k = x.shape
    _, n = w.shape
    x = x.astype(BF16)
    w = w.astype(BF16)
    row = pl.BlockSpec((tm, n), lambda i: (i, 0))
    vec = pl.BlockSpec((1, n), lambda i: (0, 0))
    return pl.pallas_call(
        _mm_norm_res_kernel,
        grid=(m // tm,),
        in_specs=[pl.BlockSpec((tm, k), lambda i: (i, 0)), pl.BlockSpec((k, n), lambda i: (0, 0)), vec, row, vec],
        out_specs=[row, row],
        out_shape=[jax.ShapeDtypeStruct((m, n), F32), jax.ShapeDtypeStruct((m, n), BF16)],
        compiler_params=_cparams(("parallel",)),
        name="matmul_norm_residual",
    )(x, w, g.reshape(1, n), res, g_next.reshape(1, n))


DSA_TQ = 128
DSA_CK = 512


def _dsa_mask_kernel(qc_ref, kc_ref, wt_ref, tri_ref, bias_ref, key_ref, *, seq, topk):
    tq, ck = DSA_TQ, DSA_CK
    qb = pl.program_id(1)
    nch = qb // (ck // tq) + 1
    qpos = qb * tq + lax.broadcasted_iota(jnp.int32, (ck, tq), 1)
    kiota = lax.broadcasted_iota(jnp.int32, (ck, tq), 0)
    qcat = qc_ref[0].reshape(IDX_HEADS * tq, 3 * IDX_HEAD_DIM)
    wrows = [wt_ref[0, h:h + 1, :] for h in range(IDX_HEADS)]

    def score_chunk(c, carry):
        off = pl.multiple_of(c * ck, ck)
        d = _dot(kc_ref[0, pl.ds(off, ck), :], qcat, _NT)
        s = jnp.zeros((ck, tq), F32)
        for h in range(IDX_HEADS):
            s = s + jnp.maximum(d[:, h * tq:(h + 1) * tq], 0.0) * wrows[h]
        s = s + 0.0
        bits = pltpu.bitcast(s, jnp.int32)
        key = bits ^ ((bits >> 31) & jnp.int32(0x7FFFFFFF))
        key_ref[pl.ds(off, ck), :] = jnp.where(off + kiota <= qpos, key, INT32_MIN)
        return carry

    lax.fori_loop(0, nch, score_chunk, 0)

    def count(pred):
        def body(c, acc):
            off = pl.multiple_of(c * ck, ck)
            m = jnp.where(pred(key_ref[pl.ds(off, ck), :]), 1, 0)
            return acc + jnp.sum(m.reshape(ck // SUBLANE, SUBLANE, tq), axis=0)
        acc = lax.fori_loop(0, nch, body, jnp.zeros((SUBLANE, tq), jnp.int32))
        return jnp.sum(acc, axis=0, keepdims=True)

    def bit_body(i, lo):
        cand = lo + (jnp.int32(1) << (31 - i))
        return jnp.where(count(lambda k: k >= cand) >= topk, cand, lo)

    tau = lax.fori_loop(0, 32, bit_body, jnp.full((1, tq), INT32_MIN, jnp.int32))
    n_gt = count(lambda k: k > tau)
    n_ge = count(lambda k: k >= tau)
    need = (topk - n_gt).astype(F32)
    few = qb * tq + lax.broadcasted_iota(jnp.int32, (1, tq), 1) + 1 <= topk
    no_cut = jnp.min(jnp.where(jnp.logical_or(n_ge == topk, few), 1, 0)) == 1

    def emit(c, take):
        off = pl.multiple_of(c * ck, ck)
        bias_ref[0, 0, pl.ds(off, ck), :] = jnp.where(take, 0.0, -jnp.inf).astype(bias_ref.dtype)

    @pl.when(no_cut)
    def _():
        def chunk(c, carry):
            off = pl.multiple_of(c * ck, ck)
            k = key_ref[pl.ds(off, ck), :]
            emit(c, jnp.logical_and(k >= tau, off + kiota <= qpos))
            return carry
        lax.fori_loop(0, nch, chunk, 0)

    @pl.when(jnp.logical_not(no_cut))
    def _():
        def chunk(c, run):
            off = pl.multiple_of(c * ck, ck)
            k = key_ref[pl.ds(off, ck), :]
            eq = jnp.logical_and(k == tau, off + kiota <= qpos)
            eqf = jnp.where(eq, 1.0, 0.0)
            before = _dot(tri_ref[...], eqf.astype(BF16)) + run
            emit(c, jnp.logical_or(k > tau, jnp.logical_and(eq, before < need)))
            return run + jnp.sum(eqf, axis=0, keepdims=True)
        lax.fori_loop(0, nch, chunk, jnp.zeros((1, tq), F32))

    def fill_chunk(c, carry):
        off = pl.multiple_of(c * ck, ck)
        bias_ref[0, 0, pl.ds(off, ck), :] = jnp.full((ck, tq), -jnp.inf, bias_ref.dtype)
        return carry

    lax.fori_loop(nch, seq // ck, fill_chunk, 0)


def _dsa_mask(idx, topk):
    bsz, seq, _ = idx.shape
    tq, ck = DSA_TQ, DSA_CK
    wi0 = IDX_Q_WIDTH + IDX_HEAD_DIM
    q_hi, q_lo = _split2(idx[:, :, :IDX_Q_WIDTH].reshape(bsz, seq, IDX_HEADS, IDX_HEAD_DIM))
    qcat = jnp.concatenate([q_hi, q_hi, q_lo], axis=-1).swapaxes(1, 2)
    k_hi, k_lo = _split2(idx[:, :, IDX_Q_WIDTH:wi0])
    kcat = jnp.concatenate([k_hi, k_lo, k_hi], axis=-1)
    wt = (idx[:, :, wi0:wi0 + IDX_HEADS] * IDX_HEAD_DIM ** -0.5).swapaxes(1, 2)
    wt = jnp.pad(wt, ((0, 0), (0, SUBLANE - IDX_HEADS), (0, 0)))
    tri = jnp.asarray(np.tril(np.ones((ck, ck), np.float32), -1), BF16)
    return pl.pallas_call(
        functools.partial(_dsa_mask_kernel, seq=seq, topk=topk),
        grid=(bsz, seq // tq),
        in_specs=[pl.BlockSpec((1, IDX_HEADS, tq, 3 * IDX_HEAD_DIM), lambda b, q: (b, 0, q, 0)),
                  pl.BlockSpec((1, seq, 3 * IDX_HEAD_DIM), lambda b, q: (b, 0, 0)),
                  pl.BlockSpec((1, SUBLANE, tq), lambda b, q: (b, 0, q)),
                  pl.BlockSpec((ck, ck), lambda b, q: (0, 0))],
        out_specs=pl.BlockSpec((1, 1, seq, tq), lambda b, q: (b, q, 0, 0)),
        out_shape=jax.ShapeDtypeStruct((bsz, seq // tq, seq, tq), BF16),
        scratch_shapes=[pltpu.VMEM((seq, tq), jnp.int32)],
        compiler_params=_cparams(("parallel", "parallel")),
        name="dsa_topk_mask",
    )(qcat, kcat, wt, tri)


ATT_TQ = 256
ATT_TK = 512


def _dsa_attn_kernel(q_ref, k_ref, vt_ref, b_ref, o_ref, m_sc, acc_sc, bias_sc):
    tq, tk, hd = ATT_TQ, ATT_TK, ATT_HEAD_DIM
    he = hd + SUBLANE
    qb = pl.program_id(1)
    kb = pl.program_id(2)

    @pl.when(kb == 0)
    def _():
        m_sc[...] = jnp.full(m_sc.shape, -jnp.inf, F32)
        acc_sc[...] = jnp.zeros(acc_sc.shape, F32)

    @pl.when(kb * tk <= qb * tq + (tq - 1))
    def _():
        for j in range(tq // DSA_TQ):
            bias_sc[:, j * DSA_TQ:(j + 1) * DSA_TQ] = b_ref[0, j].astype(F32)
        ones = jnp.ones((SUBLANE, tk), BF16)
        hs = range(ATT_HEADS)
        cs = [slice(h * hd, (h + 1) * hd) for h in hs]
        s = [_dot(k_ref[0, :, cs[h]], q_ref[0, :, cs[h]], _NT) + bias_sc[...] for h in hs]
        m_prev = [m_sc[h] for h in hs]
        m_new = [jnp.maximum(m_prev[h], jnp.max(s[h], axis=0, keepdims=True)) for h in hs]
        m_use = [jnp.where(m_new[h] == -jnp.inf, 0.0, m_new[h]) for h in hs]
        p = [jnp.exp2(s[h] - m_use[h]).astype(BF16) for h in hs]
        alpha = [jnp.exp2(m_prev[h] - m_use[h]) for h in hs]
        pv = [_dot(jnp.concatenate([vt_ref[cs[h], :], ones], axis=0), p[h]) for h in hs]
        for h in hs:
            rs = slice(h * he, (h + 1) * he)
            acc_sc[rs, :] = alpha[h] * acc_sc[rs, :] + pv[h]
            m_sc[h] = m_new[h]

    @pl.when(kb == pl.num_programs(2) - 1)
    def _():
        for h in range(ATT_HEADS):
            o_t = acc_sc[h * he:h * he + hd, :] / acc_sc[h * he + hd:h * he + hd + 1, :]
            o_ref[0, :, h * hd:(h + 1) * hd] = jnp.transpose(o_t).astype(o_ref.dtype)


def _dsa_attend(qk, vt, bias_t, out_dtype=BF16):
    bsz, seq, _ = qk.shape
    width = vt.shape[0]
    tq, tk = ATT_TQ, ATT_TK
    nkb = seq // tk

    def kb_of(i, j):
        return jnp.minimum(j, (i * tq + tq - 1) // tk)

    return pl.pallas_call(
        _dsa_attn_kernel,
        grid=(bsz, seq // tq, nkb),
        in_specs=[pl.BlockSpec((1, tq, width), lambda b, i, j: (b, i, 0)),
                  pl.BlockSpec((1, tk, width), lambda b, i, j: (b, kb_of(i, j), 1)),
                  pl.BlockSpec((width, tk), lambda b, i, j: (0, b * nkb + kb_of(i, j))),
                  pl.BlockSpec((1, tq // DSA_TQ, tk, DSA_TQ), lambda b, i, j: (b, i, kb_of(i, j), 0))],
        out_specs=pl.BlockSpec((1, tq, width), lambda b, i, j: (b, i, 0)),
        out_shape=jax.ShapeDtypeStruct((bsz, seq, width), out_dtype),
        scratch_shapes=[pltpu.VMEM((ATT_HEADS, 1, tq), F32),
                        pltpu.VMEM((ATT_HEADS * (ATT_HEAD_DIM + SUBLANE), tq), F32),
                        pltpu.VMEM((tk, tq), F32)],
        compiler_params=_cparams(("parallel", "parallel", "arbitrary")),
        name="dsa_attention",
    )(qk, qk, vt, bias_t)


RW_L = 64
RW_SEG = 256


def _head_sums(x, bd_ref):
    rows = x.shape[0]
    hi, lo = _split2(x)
    st = jnp.concatenate([hi, lo], axis=0)
    y = jnp.concatenate([_dot(st[:, g * RW_SEG:(g + 1) * RW_SEG], bd_ref[...])
                         for g in range(RWKV_WIDTH // RW_SEG)], axis=1)
    return y[:rows] + y[rows:]


def _rwkv_kernel(p_ref, halo_ref, mu_ref, w0_ref, w2_ref, a0_ref, a2_ref, g2_ref, kk_ref, ka_ref, rk_ref,
                 lng_ref, lnb_ref, tril_ref, bd_ref, o_ref, s_ref):
    L, hd, W = RW_L, RWKV_HEAD, RWKV_WIDTH
    c = pl.program_id(1)

    @pl.when(c == 0)
    def _():
        s_ref[...] = jnp.zeros(s_ref.shape, F32)

    p = p_ref[0]
    first = halo_ref[0, SUBLANE - 1:SUBLANE, :] * jnp.where(c > 0, 1.0, 0.0)
    row = lax.broadcasted_iota(jnp.int32, p.shape, 0)
    prev = jnp.where(row == 0, first, pltpu.roll(p, 1, 0))
    p = p + (prev - p) * mu_ref[...]
    r = p[:, 0:W]
    k = p[:, W:2 * W]
    v = p[:, 2 * W:3 * W]
    o_lora = 3 * W
    wl = p[:, o_lora:o_lora + DECAY_LORA]
    al = p[:, o_lora + DECAY_LORA:o_lora + DECAY_LORA + AAA_LORA]
    gl = p[:, o_lora + DECAY_LORA + AAA_LORA:]
    w = -_softplus(-(w0_ref[...] + _dot(jnp.tanh(wl).astype(BF16), w2_ref[...]))) - 0.5
    lw = -jnp.exp(w)
    a = _sigmoid(a0_ref[...] + _dot(al.astype(BF16), a2_ref[...]))
    gate = _dot(_sigmoid(gl).astype(BF16), g2_ref[...])
    khat = k * kk_ref[...]
    kk = khat / jnp.maximum(jnp.sqrt(_head_sums(khat * khat, bd_ref)), 1e-12)
    k = k * (1.0 + (a - 1.0) * ka_ref[...])
    av = -kk
    bv = kk * a

    tril = tril_ref[...]
    l1, l2, l3 = _split3(lw)
    cum = _dot(tril, l1) + (_dot(tril, l2) + _dot(tril, l3))
    tot = cum[L - 1:L, :]
    e_neg = jnp.exp(-cum)
    e_end = jnp.exp(tot - cum)
    w_tot = jnp.exp(tot)
    rt = r * jnp.exp(cum)
    at = av * jnp.exp(cum - lw)
    bt = bv * e_neg
    kt = k * e_neg
    bh = bv * e_end
    kh = k * e_end

    rw = lax.broadcasted_iota(jnp.int32, (2 * L, L), 0)
    cl = lax.broadcasted_iota(jnp.int32, (2 * L, L), 1)
    keep = cl < jnp.where(rw < L, rw, rw - (L - 1))
    eye = lax.broadcasted_iota(jnp.int32, (L, L), 0) == lax.broadcasted_iota(jnp.int32, (L, L), 1)

    hs = range(RWKV_HEADS)
    cs = [slice(h * hd, (h + 1) * hd) for h in hs]
    lhs = [jnp.concatenate([at[:, s], rt[:, s]], axis=0) for s in cs]
    a_b = [jnp.where(keep, _dot_b(lhs[h], bt[:, cs[h]], _NT), 0.0) for h in hs]
    a_k = [jnp.where(keep, _dot_b(lhs[h], kt[:, cs[h]], _NT), 0.0) for h in hs]
    n = [a_b[h][:L] for h in hs]
    ident = jnp.where(eye, 1.0, 0.0)
    t_inv = [ident + n[h] for h in hs]
    for _ in range(int(math.log2(L)) - 1):
        n = [_dot_b(n[h], n[h]) for h in hs]
        t_inv = [t_inv[h] + _dot_b(t_inv[h], n[h]) for h in hs]
    akv = [_dot_b(a_k[h], v[:, cs[h]]) for h in hs]
    p12 = [_dot_b(t_inv[h], jnp.concatenate([at[:, cs[h]], akv[h][:L]], axis=1)) for h in hs]
    q12 = [_dot_b(a_b[h][L:], p12[h]) + jnp.concatenate([rt[:, cs[h]], akv[h][L:]], axis=1) for h in hs]
    s = [s_ref[h] for h in hs]
    y = jnp.concatenate([_dot_b(q12[h][:, :hd], s[h], _NT) + q12[h][:, hd:] for h in hs], axis=1)
    g = [_dot_b(p12[h], bh[:, cs[h]], _TN) for h in hs]
    vk = [_dot_b(v[:, cs[h]], kh[:, cs[h]], _TN) for h in hs]
    s_new = [s[h] * w_tot[:, cs[h]] + _dot_b(s[h], g[h][:hd]) + g[h][hd:] + vk[h] for h in hs]
    for h in hs:
        s_ref[h] = s_new[h]

    inv_hd = 1.0 / hd
    mean = _head_sums(y, bd_ref) * inv_hd
    yc = y - mean
    var = _head_sums(yc * yc, bd_ref) * inv_hd
    yn = yc * lax.rsqrt(var + RWKV_GN_EPS) * lng_ref[...] + lnb_ref[...]
    bonus = _head_sums(r * k * rk_ref[...], bd_ref) * v
    o_ref[0] = ((yn + bonus) * gate).astype(o_ref.dtype)


def _rwkv_branch(pb, mu, w0, w2, a0, a2, g2, k_k, k_a, r_k, lnx_g, lnx_b, out_dtype=BF16):
    bsz, seq, cols = pb.shape
    L, W = RW_L, RWKV_WIDTH
    tril = jnp.asarray(np.tril(np.ones((L, L), np.float32)), BF16)
    seg = np.arange(RW_SEG) // RWKV_HEAD
    bd = jnp.asarray((seg[:, None] == seg[None, :]).astype(np.float32), BF16)

    def vec(a):
        return a.reshape(1, -1).astype(F32)

    def full(shape):
        return pl.BlockSpec(shape, lambda b, c: (0,) * len(shape))

    return pl.pallas_call(
        _rwkv_kernel,
        grid=(bsz, seq // L),
        in_specs=[pl.BlockSpec((1, L, cols), lambda b, c: (b, c, 0)),
                  pl.BlockSpec((1, SUBLANE, cols), lambda b, c: (b, jnp.maximum(c * (L // SUBLANE) - 1, 0), 0)),
                  full((1, cols)), full((1, W)), full((DECAY_LORA, W)), full((1, W)), full((AAA_LORA, W)),
                  full((GATE_LORA, W)), full((1, W)), full((1, W)), full((1, W)), full((1, W)), full((1, W)),
                  full((L, L)), full((RW_SEG, RW_SEG))],
        out_specs=pl.BlockSpec((1, L, W), lambda b, c: (b, c, 0)),
        out_shape=jax.ShapeDtypeStruct((bsz, seq, W), out_dtype),
        scratch_shapes=[pltpu.VMEM((RWKV_HEADS, RWKV_HEAD, RWKV_HEAD), F32)],
        compiler_params=_cparams(("parallel", "arbitrary")),
        name="rwkv7_time_mix",
    )(pb, pb, vec(mu), vec(w0), w2.astype(BF16), vec(a0), a2.astype(BF16), g2.astype(BF16), vec(k_k), vec(k_a),
      vec(r_k), vec(lnx_g), vec(lnx_b), tril, bd)


def _mamba_kernel(p_ref, halo_ref, cw_ref, cb_ref, dtb_ref, alog_ref, dexp_ref, ng_ref, e_ref, triu_ref,
                  o_ref, h_ref, y_sc):
    L, hd, ns = SSM_CHUNK, SSM_HEAD_DIM, SSM_STATE
    hg = SSM_HEADS // SSM_GROUPS
    gw = hg * hd
    c = pl.program_id(1)

    @pl.when(c == 0)
    def _():
        h_ref[...] = jnp.zeros(h_ref.shape, F32)

    x0 = SSM_INNER
    xr = p_ref[0, :, x0:x0 + SSM_CONV_CH]
    halo = halo_ref[0, :, x0:x0 + SSM_CONV_CH] * jnp.where(c > 0, 1.0, 0.0)
    row8 = lax.broadcasted_iota(jnp.int32, halo.shape, 0)
    acc = xr * cw_ref[SSM_CONV - 1:SSM_CONV, :] + cb_ref[...]
    for s in range(1, SSM_CONV):
        rolled = pltpu.roll(xr, s, 0)
        top = jnp.where(row8 < s, pltpu.roll(halo, s, 0), rolled[0:SUBLANE])
        shifted = jnp.concatenate([top, rolled[SUBLANE:]], axis=0)
        acc = acc + shifted * cw_ref[SSM_CONV - 1 - s:SSM_CONV - s, :]
    xbc = acc * _sigmoid(acc)
    xs = xbc[:, :SSM_INNER]
    bm = xbc[:, SSM_INNER:SSM_INNER + SSM_GROUPS * ns]
    cm = xbc[:, SSM_INNER + SSM_GROUPS * ns:]

    dt = _softplus(p_ref[0, :, x0 + SSM_CONV_CH:x0 + SSM_CONV_CH + SSM_HEADS] + dtb_ref[...])
    a = dt * (-jnp.exp(alog_ref[...]))
    triu = triu_ref[...]
    a_cum_t = _dot_exact_rhs(_split3(a), triu, _TN)
    a_cum_e = _dot_exact_rhs(_split3(a_cum_t), e_ref[...], _TN)
    dt_e = _dot_exact_rhs(_split3(dt), e_ref[...])
    xdt = xs * dt_e
    tot = a_cum_e[L - 1:L, :]
    e_in = jnp.exp(a_cum_e)
    xw = xdt * jnp.exp(tot - a_cum_e)
    e_tot = jnp.exp(tot)
    lower = lax.broadcasted_iota(jnp.int32, (L, L), 1) <= lax.broadcasted_iota(jnp.int32, (L, L), 0)

    for g in range(SSM_GROUPS):
        gs = slice(g * gw, (g + 1) * gw)
        bg = bm[:, g * ns:(g + 1) * ns]
        cg = cm[:, g * ns:(g + 1) * ns]
        cb = _dot_b(cg, bg, _NT)
        ht = h_ref[g]
        y_off = _dot_b(cg, ht) * e_in[:, gs]
        for j in range(hg):
            h = g * hg + j
            hsl = slice(h * hd, (h + 1) * hd)
            seg = a_cum_e[:, h * hd:h * hd + 1] - a_cum_t[h:h + 1, :]
            m = cb * jnp.exp(jnp.where(lower, seg, -jnp.inf))
            y_sc[:, hsl] = _dot_b(m, xdt[:, hsl]) + y_off[:, j * hd:(j + 1) * hd]
        h_ref[g] = ht * e_tot[:, gs] + _dot_b(bg, xw[:, gs], _TN)

    z = p_ref[0, :, 0:SSM_INNER]
    y = (y_sc[...] + xs * dexp_ref[...]) * (z * _sigmoid(z))
    for g in range(SSM_GROUPS):
        gs = slice(g * gw, (g + 1) * gw)
        o_ref[0, :, gs] = _rms(y[:, gs], ng_ref[:, gs]).astype(o_ref.dtype)


def _mamba_branch(pc, conv_w, conv_b, dt_bias, a_log, d_skip, norm_g, out_dtype=BF16):
    bsz, seq, cols = pc.shape
    L = SSM_CHUNK
    triu = jnp.asarray(np.triu(np.ones((L, L), np.float32)), BF16)
    expand = jnp.asarray(np.repeat(np.eye(SSM_HEADS, dtype=np.float32), SSM_HEAD_DIM, axis=1), BF16)

    def full(shape):
        return pl.BlockSpec(shape, lambda b, c: (0,) * len(shape))

    return pl.pallas_call(
        _mamba_kernel,
        grid=(bsz, seq // L),
        in_specs=[pl.BlockSpec((1, L, cols), lambda b, c: (b, c, 0)),
                  pl.BlockSpec((1, SUBLANE, cols), lambda b, c: (b, jnp.maximum(c * (L // SUBLANE) - 1, 0), 0)),
                  full((SSM_CONV, SSM_CONV_CH)), full((1, SSM_CONV_CH)), full((1, SSM_HEADS)), full((1, SSM_HEADS)),
                  full((1, SSM_INNER)), full((1, SSM_INNER)), full((SSM_HEADS, SSM_INNER)), full((L, L))],
        out_specs=pl.BlockSpec((1, L, SSM_INNER), lambda b, c: (b, c, 0)),
        out_shape=jax.ShapeDtypeStruct((bsz, seq, SSM_INNER), out_dtype),
        scratch_shapes=[pltpu.VMEM((SSM_GROUPS, SSM_STATE, SSM_INNER // SSM_GROUPS), F32),
                        pltpu.VMEM((L, SSM_INNER), F32)],
        compiler_params=_cparams(("parallel", "arbitrary")),
        name="mamba2_mixer",
    )(pc, pc, conv_w, conv_b.reshape(1, -1), dt_bias.reshape(1, -1), a_log.reshape(1, -1),
      jnp.repeat(d_skip, SSM_HEAD_DIM).reshape(1, -1), norm_g.reshape(1, -1), expand, triu)


def _merge_kernel(oa_ref, ob_ref, oc_ref, pg_ref, x_ref, wa_ref, wb_ref, wc_ref, wo_ref, g1_ref, g2_ref,
                  xo_ref, hf_ref):
    d = D_MODEL
    merged = (_sigmoid(pg_ref[:, 0:d].astype(F32)) * _dot(oa_ref[...], wa_ref[...])
              + _sigmoid(pg_ref[:, d:2 * d].astype(F32)) * _dot(ob_ref[...], wb_ref[...])
              + _sigmoid(pg_ref[:, 2 * d:3 * d].astype(F32)) * _dot(oc_ref[...], wc_ref[...]))
    x = x_ref[...] + _rms(_dot(merged.astype(BF16), wo_ref[...]), g1_ref[...])
    xo_ref[...] = x
    hf_ref[...] = _rms(x, g2_ref[...]).astype(hf_ref.dtype)


def _merge_out(o_a, o_b, o_c, pg, x, w_o_att, w_o_rwkv, w_o_ssm, w_out, post_mix_g, pre_ffn_g, tm=256):
    n, d = x.shape

    def row(width):
        return pl.BlockSpec((tm, width), lambda i: (i, 0))

    def full(shape):
        return pl.BlockSpec(shape, lambda i: (0, 0))

    return pl.pallas_call(
        _merge_kernel,
        grid=(n // tm,),
        in_specs=[row(ATT_WIDTH), row(RWKV_WIDTH), row(SSM_INNER), row(GATE_COLS), row(d),
                  full((ATT_WIDTH, d)), full((RWKV_WIDTH, d)), full((SSM_INNER, d)), full((d, d)),
                  full((1, d)), full((1, d))],
        out_specs=[row(d), row(d)],
        out_shape=[jax.ShapeDtypeStruct((n, d), F32), jax.ShapeDtypeStruct((n, d), BF16)],
        compiler_params=_cparams(("parallel",)),
        name="merge_out",
    )(o_a, o_b, o_c, pg, x, w_o_att.astype(BF16), w_o_rwkv.astype(BF16), w_o_ssm.astype(BF16),
      w_out.astype(BF16), post_mix_g.reshape(1, d), pre_ffn_g.reshape(1, d))


W_IN_PIECES = (
    (0, 2 * ATT_WIDTH, 2 * ATT_WIDTH),
    (2 * ATT_WIDTH, ATT_WIDTH, ATT_WIDTH),
    (3 * ATT_WIDTH, A_COLS - 3 * ATT_WIDTH, IDX_COLS),
    (A_COLS, B_COLS, B_COLS),
    (A_COLS + B_COLS, C_COLS, C_COLS_PAD),
    (A_COLS + B_COLS + C_COLS, GATE_COLS, GATE_COLS),
)
W_SPLIT_ROWS = 128


def _split_w_kernel(w_ref, *out_refs):
    for (start, width, padded), o_ref in zip(W_IN_PIECES, out_refs):
        piece = w_ref[:, start:start + width].astype(o_ref.dtype)
        if padded > width:
            piece = jnp.concatenate([piece, jnp.zeros((piece.shape[0], padded - width), o_ref.dtype)], axis=1)
        o_ref[...] = piece


def _split_w_in(w_in):
    k, cols = w_in.shape
    tk = W_SPLIT_ROWS
    return pl.pallas_call(
        _split_w_kernel,
        grid=(k // tk,),
        in_specs=[pl.BlockSpec((tk, cols), lambda i: (i, 0))],
        out_specs=[pl.BlockSpec((tk, p[2]), lambda i: (i, 0)) for p in W_IN_PIECES],
        out_shape=[jax.ShapeDtypeStruct((k, p[2]), BF16) for p in W_IN_PIECES],
        compiler_params=_cparams(("parallel",)),
        name="split_w_in",
    )(w_in)


def _layer(x, h, bsz, seq, g_next, w_in, w_o_att, rwkv_mu, rwkv_w0, rwkv_w2, rwkv_a0, rwkv_a2, rwkv_g2,
           rwkv_k_k, rwkv_k_a, rwkv_r_k, rwkv_lnx_g, rwkv_lnx_b, w_o_rwkv, ssm_conv_w, ssm_conv_b,
           ssm_dt_bias, ssm_a_log, ssm_d, ssm_norm_g, w_o_ssm, w_out, post_mix_g, pre_ffn_g,
           w_ff1, w_ff2, post_ffn_g):
    n = bsz * seq
    topk = min(TOPK_MAX, seq // 4)
    w_qk, w_v, w_idx, w_b, w_c, w_g = _split_w_in(w_in)

    q_scale = ATT_HEAD_DIM ** -0.5 * math.log2(math.e)
    tq_ = _rope_tables(seq, ATT_HEAD_DIM, 1, ATT_ROT, LANE, q_scale)
    tk_ = _rope_tables(seq, ATT_HEAD_DIM, 1, ATT_ROT, LANE)
    qk = _mm_rope(h, w_qk, tuple(jnp.stack([a, b]) for a, b in zip(tq_, tk_)), ATT_ROT // 2,
                  BF16, ATT_WIDTH, seq).reshape(bsz, seq, 2 * ATT_WIDTH)
    vt = _mm_t(w_v.T, h)
    ti = list(_rope_tables(seq, IDX_HEAD_DIM, IDX_HEADS + 1, IDX_ROT, IDX_COLS))
    wi0 = IDX_Q_WIDTH + IDX_HEAD_DIM
    ti[0] = ti[0].at[:, wi0:wi0 + IDX_HEADS].set(IDX_HEADS ** -0.5)
    idx = _mm_rope(h, w_idx, tuple(t[None] for t in ti), IDX_ROT // 2,
                   F32, IDX_COLS, seq).reshape(bsz, seq, IDX_COLS)
    mask = _dsa_mask(idx, topk)
    o_a = _dsa_attend(qk, vt, mask).reshape(n, ATT_WIDTH)

    pb = _mm(h, w_b, F32).reshape(bsz, seq, B_COLS)
    o_b = _rwkv_branch(pb, rwkv_mu, rwkv_w0, rwkv_w2, rwkv_a0, rwkv_a2, rwkv_g2, rwkv_k_k, rwkv_k_a, rwkv_r_k,
                       rwkv_lnx_g, rwkv_lnx_b).reshape(n, RWKV_WIDTH)
    pc = _mm(h, w_c, F32).reshape(bsz, seq, C_COLS_PAD)
    o_c = _mamba_branch(pc, ssm_conv_w, ssm_conv_b, ssm_dt_bias, ssm_a_log, ssm_d, ssm_norm_g).reshape(n, SSM_INNER)

    pg = _mm(h, w_g, BF16)
    x, hf = _merge_out(o_a, o_b, o_c, pg, x, w_o_att, w_o_rwkv, w_o_ssm, w_out, post_mix_g, pre_ffn_g)
    f = _mm(hf, w_ff1, out_dtype=BF16, act="relu2")
    return _mm_norm_res(f, w_ff2, post_ffn_g, x, g_next)


def kernel(x, pre_mix_g, w_in, w_o_att, rwkv_mu, rwkv_w0, rwkv_w2, rwkv_a0, rwkv_a2, rwkv_g2, rwkv_k_k,
           rwkv_k_a, rwkv_r_k, rwkv_lnx_g, rwkv_lnx_b, w_o_rwkv, ssm_conv_w, ssm_conv_b, ssm_dt_bias,
           ssm_a_log, ssm_d, ssm_norm_g, w_o_ssm, w_out, post_mix_g, pre_ffn_g, w_ff1, w_ff2, post_ffn_g):
    bsz, seq, d = x.shape
    depth = pre_mix_g.shape[0]
    params = (w_in, w_o_att, rwkv_mu, rwkv_w0, rwkv_w2, rwkv_a0, rwkv_a2, rwkv_g2, rwkv_k_k,
              rwkv_k_a, rwkv_r_k, rwkv_lnx_g, rwkv_lnx_b, w_o_rwkv, ssm_conv_w, ssm_conv_b, ssm_dt_bias,
              ssm_a_log, ssm_d, ssm_norm_g, w_o_ssm, w_out, post_mix_g, pre_ffn_g, w_ff1, w_ff2, post_ffn_g)
    y = x.reshape(bsz * seq, d)
    h = _rmsnorm(y, pre_mix_g[0])
    for i in range(depth):
        g_next = pre_mix_g[min(i + 1, depth - 1)]
        y, h = _layer(y, h, bsz, seq, g_next, *[p[i] for p in params])
    return y.reshape(bsz, seq, d)
```

```python
import functools
import math

import jax
import jax.numpy as jnp
import numpy as np
from jax import lax
from jax.experimental import pallas as pl
from jax.experimental.pallas import tpu as pltpu

F32 = jnp.float32
BF16 = jnp.bfloat16

D_MODEL = 1024
ATT_HEADS = 8
ATT_HEAD_DIM = 128
ATT_WIDTH = ATT_HEADS * ATT_HEAD_DIM
IDX_HEADS = 4
IDX_HEAD_DIM = 64
IDX_Q_WIDTH = IDX_HEADS * IDX_HEAD_DIM
TOPK_MAX = 256
ROPE_THETA = 500000.0
ATT_ROT = ATT_HEAD_DIM // 4
IDX_ROT = IDX_HEAD_DIM // 4
RWKV_HEAD = 64
RWKV_WIDTH = D_MODEL
RWKV_HEADS = RWKV_WIDTH // RWKV_HEAD
DECAY_LORA = 64
AAA_LORA = 64
GATE_LORA = 128
RWKV_GN_EPS = 64e-5
SSM_INNER = 2 * D_MODEL
SSM_HEAD_DIM = 64
SSM_HEADS = SSM_INNER // SSM_HEAD_DIM
SSM_GROUPS = 2
SSM_STATE = 128
SSM_CONV = 4
SSM_CHUNK = 128
SSM_CONV_CH = SSM_INNER + 2 * SSM_GROUPS * SSM_STATE
D_FF = 4 * D_MODEL
N_BRANCH = 3
NORM_EPS = 1e-6

A_SIZES = [ATT_WIDTH] * 3 + [IDX_Q_WIDTH, IDX_HEAD_DIM, IDX_HEADS]
A_COLS = sum(A_SIZES)
B_COLS = 3 * RWKV_WIDTH + DECAY_LORA + AAA_LORA + GATE_LORA
C_COLS = SSM_INNER + SSM_CONV_CH + SSM_HEADS
GATE_COLS = N_BRANCH * D_MODEL

V7X_VMEM_BYTES = 64 * 1024 * 1024
VMEM_LIMIT = 56 * 1024 * 1024
LANE = 128
SUBLANE = 8

IDX_COLS = 3 * LANE
C_COLS_PAD = 38 * LANE

INT32_MIN = np.int32(-(2 ** 31))


def _cparams(sem):
    return pltpu.CompilerParams(dimension_semantics=sem, vmem_limit_bytes=VMEM_LIMIT)


def _split2(a):
    hi = a.astype(BF16)
    lo = (a - hi.astype(F32)).astype(BF16)
    return hi, lo


def _split3(a):
    h1 = a.astype(BF16)
    r1 = a - h1.astype(F32)
    h2 = r1.astype(BF16)
    h3 = (r1 - h2.astype(F32)).astype(BF16)
    return h1, h2, h3


_NN = (((1,), (0,)), ((), ()))
_NT = (((1,), (1,)), ((), ()))
_TN = (((0,), (0,)), ((), ()))


def _dot(a, b, dims=_NN):
    return lax.dot_general(a, b, dims, preferred_element_type=F32)


def _dot_b(a, b, dims=_NN):
    return _dot(a.astype(BF16), b.astype(BF16), dims)


def _dot_exact_rhs(parts, mat, dims=_NN):
    out = _dot(parts[0], mat, dims)
    for p in parts[1:]:
        out = out + _dot(p, mat, dims)
    return out


def _softplus(x):
    return jnp.maximum(x, 0.0) + jnp.log(1.0 + jnp.exp(-jnp.abs(x)))


def _sigmoid(x):
    return 1.0 / (1.0 + jnp.exp(-x))


def _rms(y, g):
    return y * lax.rsqrt(jnp.mean(y * y, axis=-1, keepdims=True) + NORM_EPS) * g


def _rmsnorm_kernel(x_ref, g_ref, o_ref):
    o_ref[...] = _rms(x_ref[...], g_ref[...]).astype(o_ref.dtype)


def _rmsnorm(x, g, out_dtype=BF16, tm=1024):
    n, d = x.shape
    return pl.pallas_call(
        _rmsnorm_kernel,
        grid=(n // tm,),
        in_specs=[pl.BlockSpec((tm, d), lambda i: (i, 0)), pl.BlockSpec((1, d), lambda i: (0, 0))],
        out_specs=pl.BlockSpec((tm, d), lambda i: (i, 0)),
        out_shape=jax.ShapeDtypeStruct((n, d), out_dtype),
        compiler_params=_cparams(("parallel",)),
        name="rmsnorm",
    )(x, g.reshape(1, d))


def _col_tile(n, cap=2560):
    best = LANE
    for t in range(LANE, min(n, cap) + 1, LANE):
        if n % t == 0:
            best = t
    return best


def _row_tile(m, cap=1024):
    for t in (1024, 512, 256, 128):
        if t <= cap and m % t == 0:
            return t
    return m


def _mm_kernel(x_ref, w_ref, o_ref, *, act):
    y = _dot(x_ref[...], w_ref[...])
    if act == "relu2":
        y = jnp.square(jnp.maximum(y, 0.0))
    o_ref[...] = y.astype(o_ref.dtype)


def _mm(x, w, out_dtype=F32, act=None, tn_cap=2560):
    m, k = x.shape
    _, n = w.shape
    x = x.astype(BF16)
    w = w.astype(BF16)
    tm = _row_tile(m, 512 if jnp.dtype(out_dtype).itemsize == 4 else 1024)
    tn = _col_tile(n, tn_cap)
    return pl.pallas_call(
        functools.partial(_mm_kernel, act=act),
        grid=(n // tn, m // tm),
        in_specs=[pl.BlockSpec((tm, k), lambda j, i: (i, 0)), pl.BlockSpec((k, tn), lambda j, i: (0, j))],
        out_specs=pl.BlockSpec((tm, tn), lambda j, i: (i, j)),
        out_shape=jax.ShapeDtypeStruct((m, n), out_dtype),
        compiler_params=_cparams(("parallel", "parallel")),
        name="matmul",
    )(x, w)


def _mm_t_kernel(wt_ref, x_ref, o_ref):
    o_ref[...] = _dot(wt_ref[...], x_ref[...], _NT).astype(o_ref.dtype)


def _mm_t(wt, x, out_dtype=BF16):
    n, k = wt.shape
    m, _ = x.shape
    tm = _row_tile(m)
    return pl.pallas_call(
        _mm_t_kernel,
        grid=(m // tm,),
        in_specs=[pl.BlockSpec((n, k), lambda i: (0, 0)), pl.BlockSpec((tm, k), lambda i: (i, 0))],
        out_specs=pl.BlockSpec((n, tm), lambda i: (0, i)),
        out_shape=jax.ShapeDtypeStruct((n, m), out_dtype),
        compiler_params=_cparams(("parallel",)),
        name="matmul_t",
    )(wt.astype(BF16), x.astype(BF16))


def _mm_rope_kernel(x_ref, w_ref, c_ref, s1_ref, s2_ref, o_ref, *, shift):
    y = _dot(x_ref[...], w_ref[...])
    tw = c_ref.shape[-1]
    for g in range(y.shape[1] // LANE):
        yg = y[:, g * LANE:(g + 1) * LANE]
        ts = slice((g * LANE) % tw, (g * LANE) % tw + LANE)
        c, s1, s2 = c_ref[0, :, ts], s1_ref[0, :, ts], s2_ref[0, :, ts]
        out = yg * c + pltpu.roll(yg, shift, 1) * s1 + pltpu.roll(yg, LANE - shift, 1) * s2
        o_ref[:, g * LANE:(g + 1) * LANE] = out.astype(o_ref.dtype)


def _mm_rope(x, w, tabs, shift, out_dtype, tn, seq):
    m, k = x.shape
    _, n = w.shape
    x = x.astype(BF16)
    w = w.astype(BF16)
    tm = _row_tile(min(m, seq), 512)
    tpb = seq // tm
    tw = tabs[0].shape[-1]
    tspec = pl.BlockSpec((1, tm, tw), lambda j, i: (j, i % tpb, 0))
    return pl.pallas_call(
        functools.partial(_mm_rope_kernel, shift=shift),
        grid=(n // tn, m // tm),
        in_specs=[pl.BlockSpec((tm, k), lambda j, i: (i, 0)), pl.BlockSpec((k, tn), lambda j, i: (0, j)),
                  tspec, tspec, tspec],
        out_specs=pl.BlockSpec((tm, tn), lambda j, i: (i, j)),
        out_shape=jax.ShapeDtypeStruct((m, n), out_dtype),
        compiler_params=_cparams(("parallel", "parallel")),
        name="matmul_rope",
    )(x, w, *tabs)


def _rope_tables(seq, head, n_heads, rot, width, scale=1.0):
    half = rot // 2
    inv = ROPE_THETA ** (-np.arange(half, dtype=np.float32) * 2.0 / rot)
    ang = jnp.arange(seq, dtype=F32)[:, None] * jnp.asarray(inv, F32)[None, :]
    cos, sin = jnp.cos(ang), jnp.sin(ang)
    zeros = jnp.zeros((seq, head - rot), F32)
    c_head = jnp.concatenate([cos, cos, zeros + 1.0], axis=1)
    s1_head = jnp.concatenate([jnp.zeros_like(sin), sin, zeros], axis=1)
    s2_head = jnp.concatenate([-sin, jnp.zeros_like(sin), zeros], axis=1)
    rest = width - head * n_heads
    c = jnp.concatenate([c_head] * n_heads + [jnp.ones((seq, rest), F32)], axis=1)
    s1 = jnp.concatenate([s1_head] * n_heads + [jnp.zeros((seq, rest), F32)], axis=1)
    s2 = jnp.concatenate([s2_head] * n_heads + [jnp.zeros((seq, rest), F32)], axis=1)
    return c * scale, s1 * scale, s2 * scale


def _mm_norm_res_kernel(x_ref, w_ref, g_ref, r_ref, g2_ref, o_ref, h_ref):
    y = r_ref[...] + _rms(_dot(x_ref[...], w_ref[...]), g_ref[...])
    o_ref[...] = y
    h_ref[...] = _rms(y, g2_ref[...]).astype(h_ref.dtype)


def _mm_norm_res(x, w, g, res, g_next, tm=512):
    m, k = x.shape
    _, n = w.shape
    x = x.astype(BF16)
    w = w.astype(BF16)
    row = pl.BlockSpec((tm, n), lambda i: (i, 0))
    vec = pl.BlockSpec((1, n), lambda i: (0, 0))
    return pl.pallas_call(
        _mm_norm_res_kernel,
        grid=(m // tm,),
        in_specs=[pl.BlockSpec((tm, k), lambda i: (i, 0)), pl.BlockSpec((k, n), lambda i: (0, 0)), vec, row, vec],
        out_specs=[row, row],
        out_shape=[jax.ShapeDtypeStruct((m, n), F32), jax.ShapeDtypeStruct((m, n), BF16)],
        compiler_params=_cparams(("parallel",)),
        name="matmul_norm_residual",
    )(x, w, g.reshape(1, n), res, g_next.reshape(1, n))


DSA_TQ = 128
DSA_CK = 512


def _dsa_mask_kernel(qc_ref, kc_ref, wt_ref, tri_ref, bias_ref, key_ref, *, seq, topk):
    tq, ck = DSA_TQ, DSA_CK
    qb = pl.program_id(1)
    nch = qb // (ck // tq) + 1
    qpos = qb * tq + lax.broadcasted_iota(jnp.int32, (ck, tq), 1)
    kiota = lax.broadcasted_iota(jnp.int32, (ck, tq), 0)
    qcat = qc_ref[0].reshape(IDX_HEADS * tq, 3 * IDX_HEAD_DIM)
    wrows = [wt_ref[0, h:h + 1, :] for h in range(IDX_HEADS)]

    def score_chunk(c, carry):
        off = pl.multiple_of(c * ck, ck)
        d = _dot(kc_ref[0, pl.ds(off, ck), :], qcat, _NT)
        s = jnp.zeros((ck, tq), F32)
        for h in range(IDX_HEADS):
            s = s + jnp.maximum(d[:, h * tq:(h + 1) * tq], 0.0) * wrows[h]
        s = s + 0.0
        bits = pltpu.bitcast(s, jnp.int32)
        key = bits ^ ((bits >> 31) & jnp.int32(0x7FFFFFFF))
        key_ref[pl.ds(off, ck), :] = jnp.where(off + kiota <= qpos, key, INT32_MIN)
        return carry

    lax.fori_loop(0, nch, score_chunk, 0)

    def count(pred):
        def body(c, acc):
            off = pl.multiple_of(c * ck, ck)
            m = jnp.where(pred(key_ref[pl.ds(off, ck), :]), 1, 0)
            return acc + jnp.sum(m.reshape(ck // SUBLANE, SUBLANE, tq), axis=0)
        acc = lax.fori_loop(0, nch, body, jnp.zeros((SUBLANE, tq), jnp.int32))
        return jnp.sum(acc, axis=0, keepdims=True)

    def bit_body(i, lo):
        cand = lo + (jnp.int32(1) << (31 - i))
        return jnp.where(count(lambda k: k >= cand) >= topk, cand, lo)

    tau = lax.fori_loop(0, 32, bit_body, jnp.full((1, tq), INT32_MIN, jnp.int32))
    n_gt = count(lambda k: k > tau)
    n_ge = count(lambda k: k >= tau)
    need = (topk - n_gt).astype(F32)
    few = qb * tq + lax.broadcasted_iota(jnp.int32, (1, tq), 1) + 1 <= topk
    no_cut = jnp.min(jnp.where(jnp.logical_or(n_ge == topk, few), 1, 0)) == 1

    def emit(c, take):
        off = pl.multiple_of(c * ck, ck)
        bias_ref[0, 0, pl.ds(off, ck), :] = jnp.where(take, 0.0, -jnp.inf).astype(bias_ref.dtype)

    @pl.when(no_cut)
    def _():
        def chunk(c, carry):
            off = pl.multiple_of(c * ck, ck)
            k = key_ref[pl.ds(off, ck), :]
            emit(c, jnp.logical_and(k >= tau, off + kiota <= qpos))
            return carry
        lax.fori_loop(0, nch, chunk, 0)

    @pl.when(jnp.logical_not(no_cut))
    def _():
        def chunk(c, run):
            off = pl.multiple_of(c * ck, ck)
            k = key_ref[pl.ds(off, ck), :]
            eq = jnp.logical_and(k == tau, off + kiota <= qpos)
            eqf = jnp.where(eq, 1.0, 0.0)
            before = _dot(tri_ref[...], eqf.astype(BF16)) + run
            emit(c, jnp.logical_or(k > tau, jnp.logical_and(eq, before < need)))
            return run + jnp.sum(eqf, axis=0, keepdims=True)
        lax.fori_loop(0, nch, chunk, jnp.zeros((1, tq), F32))

    def fill_chunk(c, carry):
        off = pl.multiple_of(c * ck, ck)
        bias_ref[0, 0, pl.ds(off, ck), :] = jnp.full((ck, tq), -jnp.inf, bias_ref.dtype)
        return carry

    lax.fori_loop(nch, seq // ck, fill_chunk, 0)


def _dsa_mask(idx, topk):
    bsz, seq, _ = idx.shape
    tq, ck = DSA_TQ, DSA_CK
    wi0 = IDX_Q_WIDTH + IDX_HEAD_DIM
    q_hi, q_lo = _split2(idx[:, :, :IDX_Q_WIDTH].reshape(bsz, seq, IDX_HEADS, IDX_HEAD_DIM))
    qcat = jnp.concatenate([q_hi, q_hi, q_lo], axis=-1).swapaxes(1, 2)
    k_hi, k_lo = _split2(idx[:, :, IDX_Q_WIDTH:wi0])
    kcat = jnp.concatenate([k_hi, k_lo, k_hi], axis=-1)
    wt = (idx[:, :, wi0:wi0 + IDX_HEADS] * IDX_HEAD_DIM ** -0.5).swapaxes(1, 2)
    wt = jnp.pad(wt, ((0, 0), (0, SUBLANE - IDX_HEADS), (0, 0)))
    tri = jnp.asarray(np.tril(np.ones((ck, ck), np.float32), -1), BF16)
    return pl.pallas_call(
        functools.partial(_dsa_mask_kernel, seq=seq, topk=topk),
        grid=(bsz, seq // tq),
        in_specs=[pl.BlockSpec((1, IDX_HEADS, tq, 3 * IDX_HEAD_DIM), lambda b, q: (b, 0, q, 0)),
                  pl.BlockSpec((1, seq, 3 * IDX_HEAD_DIM), lambda b, q: (b, 0, 0)),
                  pl.BlockSpec((1, SUBLANE, tq), lambda b, q: (b, 0, q)),
                  pl.BlockSpec((ck, ck), lambda b, q: (0, 0))],
        out_specs=pl.BlockSpec((1, 1, seq, tq), lambda b, q: (b, q, 0, 0)),
        out_shape=jax.ShapeDtypeStruct((bsz, seq // tq, seq, tq), BF16),
        scratch_shapes=[pltpu.VMEM((seq, tq), jnp.int32)],
        compiler_params=_cparams(("parallel", "parallel")),
        name="dsa_topk_mask",
    )(qcat, kcat, wt, tri)


ATT_TQ = 512
ATT_TK = 512


def _dsa_attn_kernel(qb_ref, kb_ref, q_ref, k_ref, vt_ref, b_ref, o_ref, m_sc, acc_sc, bias_sc):
    tq, tk, hd = ATT_TQ, ATT_TK, ATT_HEAD_DIM
    he = hd + SUBLANE
    step = pl.program_id(1)
    qb = qb_ref[step]
    kb = kb_ref[step]

    @pl.when(kb == 0)
    def _():
        m_sc[...] = jnp.full(m_sc.shape, -jnp.inf, F32)
        acc_sc[...] = jnp.zeros(acc_sc.shape, F32)

    for j in range(tq // DSA_TQ):
        bias_sc[:, j * DSA_TQ:(j + 1) * DSA_TQ] = b_ref[0, j].astype(F32)
    ones = jnp.ones((SUBLANE, tk), BF16)
    hs = range(ATT_HEADS)
    cs = [slice(h * hd, (h + 1) * hd) for h in hs]
    s = [_dot(k_ref[0, :, cs[h]], q_ref[0, :, cs[h]], _NT) + bias_sc[...] for h in hs]
    m_prev = [m_sc[h] for h in hs]
    m_new = [jnp.maximum(m_prev[h], jnp.max(s[h], axis=0, keepdims=True)) for h in hs]
    m_use = [jnp.where(m_new[h] == -jnp.inf, 0.0, m_new[h]) for h in hs]
    p = [jnp.exp2(s[h] - m_use[h]).astype(BF16) for h in hs]
    alpha = [jnp.exp2(m_prev[h] - m_use[h]) for h in hs]
    pv = [_dot(jnp.concatenate([vt_ref[cs[h], :], ones], axis=0), p[h]) for h in hs]
    for h in hs:
        rs = slice(h * he, (h + 1) * he)
        acc_sc[rs, :] = alpha[h] * acc_sc[rs, :] + pv[h]
        m_sc[h] = m_new[h]

    @pl.when(kb == (qb * tq + (tq - 1)) // tk)
    def _():
        for h in range(ATT_HEADS):
            o_t = acc_sc[h * he:h * he + hd, :] / acc_sc[h * he + hd:h * he + hd + 1, :]
            o_ref[0, :, h * hd:(h + 1) * hd] = jnp.transpose(o_t).astype(o_ref.dtype)


def _dsa_attend(qk, vt, bias_t, out_dtype=BF16):
    bsz, seq, _ = qk.shape
    width = vt.shape[0]
    tq, tk = ATT_TQ, ATT_TK
    nkb = seq // tk
    pairs = [(i, j) for i in range(seq // tq) for j in range((i * tq + tq - 1) // tk + 1)]
    qb_tab = jnp.asarray([p[0] for p in pairs], jnp.int32)
    kb_tab = jnp.asarray([p[1] for p in pairs], jnp.int32)
    grid_spec = pltpu.PrefetchScalarGridSpec(
        num_scalar_prefetch=2,
        grid=(bsz, len(pairs)),
        in_specs=[pl.BlockSpec((1, tq, width), lambda b, s, qt, kt: (b, qt[s], 0)),
                  pl.BlockSpec((1, tk, width), lambda b, s, qt, kt: (b, kt[s], 1)),
                  pl.BlockSpec((width, tk), lambda b, s, qt, kt: (0, b * nkb + kt[s])),
                  pl.BlockSpec((1, tq // DSA_TQ, tk, DSA_TQ), lambda b, s, qt, kt: (b, qt[s], kt[s], 0))],
        out_specs=pl.BlockSpec((1, tq, width), lambda b, s, qt, kt: (b, qt[s], 0)),
        scratch_shapes=[pltpu.VMEM((ATT_HEADS, 1, tq), F32),
                        pltpu.VMEM((ATT_HEADS * (ATT_HEAD_DIM + SUBLANE), tq), F32),
                        pltpu.VMEM((tk, tq), F32)])
    return pl.pallas_call(
        _dsa_attn_kernel,
        grid_spec=grid_spec,
        out_shape=jax.ShapeDtypeStruct((bsz, seq, width), out_dtype),
        compiler_params=_cparams(("parallel", "arbitrary")),
        name="dsa_attention",
    )(qb_tab, kb_tab, qk, qk, vt, bias_t)


RW_L = 64
RW_SEG = 256


def _head_sums(x, bd_ref):
    rows = x.shape[0]
    hi, lo = _split2(x)
    st = jnp.concatenate([hi, lo], axis=0)
    y = jnp.concatenate([_dot(st[:, g * RW_SEG:(g + 1) * RW_SEG], bd_ref[...])
                         for g in range(RWKV_WIDTH // RW_SEG)], axis=1)
    return y[:rows] + y[rows:]


def _rwkv_kernel(p_ref, halo_ref, mu_ref, w0_ref, w2_ref, a0_ref, a2_ref, g2_ref, kk_ref, ka_ref, rk_ref,
                 lng_ref, lnb_ref, tril_ref, bd_ref, o_ref, s_ref):
    L, hd, W = RW_L, RWKV_HEAD, RWKV_WIDTH
    c = pl.program_id(1)

    @pl.when(c == 0)
    def _():
        s_ref[...] = jnp.zeros(s_ref.shape, F32)

    p = p_ref[0]
    first = halo_ref[0, SUBLANE - 1:SUBLANE, :] * jnp.where(c > 0, 1.0, 0.0)
    row = lax.broadcasted_iota(jnp.int32, p.shape, 0)
    prev = jnp.where(row == 0, first, pltpu.roll(p, 1, 0))
    p = p + (prev - p) * mu_ref[...]
    r = p[:, 0:W]
    k = p[:, W:2 * W]
    v = p[:, 2 * W:3 * W]
    o_lora = 3 * W
    wl = p[:, o_lora:o_lora + DECAY_LORA]
    al = p[:, o_lora + DECAY_LORA:o_lora + DECAY_LORA + AAA_LORA]
    gl = p[:, o_lora + DECAY_LORA + AAA_LORA:]
    w = -_softplus(-(w0_ref[...] + _dot(jnp.tanh(wl).astype(BF16), w2_ref[...]))) - 0.5
    lw = -jnp.exp(w)
    a = _sigmoid(a0_ref[...] + _dot(al.astype(BF16), a2_ref[...]))
    gate = _dot(_sigmoid(gl).astype(BF16), g2_ref[...])
    khat = k * kk_ref[...]
    kk = khat / jnp.maximum(jnp.sqrt(_head_sums(khat * khat, bd_ref)), 1e-12)
    k = k * (1.0 + (a - 1.0) * ka_ref[...])
    av = -kk
    bv = kk * a

    tril = tril_ref[...]
    l1, l2, l3 = _split3(lw)
    cum = _dot(tril, l1) + (_dot(tril, l2) + _dot(tril, l3))
    tot = cum[L - 1:L, :]
    e_neg = jnp.exp(-cum)
    e_end = jnp.exp(tot - cum)
    w_tot = jnp.exp(tot)
    rt = r * jnp.exp(cum)
    at = av * jnp.exp(cum - lw)
    bt = bv * e_neg
    kt = k * e_neg
    bh = bv * e_end
    kh = k * e_end

    rw = lax.broadcasted_iota(jnp.int32, (2 * L, L), 0)
    cl = lax.broadcasted_iota(jnp.int32, (2 * L, L), 1)
    keep = cl < jnp.where(rw < L, rw, rw - (L - 1))
    eye = lax.broadcasted_iota(jnp.int32, (L, L), 0) == lax.broadcasted_iota(jnp.int32, (L, L), 1)

    hs = range(RWKV_HEADS)
    cs = [slice(h * hd, (h + 1) * hd) for h in hs]
    lhs = [jnp.concatenate([at[:, s], rt[:, s]], axis=0) for s in cs]
    a_b = [jnp.where(keep, _dot_b(lhs[h], bt[:, cs[h]], _NT), 0.0) for h in hs]
    a_k = [jnp.where(keep, _dot_b(lhs[h], kt[:, cs[h]], _NT), 0.0) for h in hs]
    n = [a_b[h][:L] for h in hs]
    ident = jnp.where(eye, 1.0, 0.0)
    t_inv = [ident + n[h] for h in hs]
    for _ in range(int(math.log2(L)) - 1):
        n = [_dot_b(n[h], n[h]) for h in hs]
        t_inv = [t_inv[h] + _dot_b(t_inv[h], n[h]) for h in hs]
    akv = [_dot_b(a_k[h], v[:, cs[h]]) for h in hs]
    p12 = [_dot_b(t_inv[h], jnp.concatenate([at[:, cs[h]], akv[h][:L]], axis=1)) for h in hs]
    q12 = [_dot_b(a_b[h][L:], p12[h]) + jnp.concatenate([rt[:, cs[h]], akv[h][L:]], axis=1) for h in hs]
    s = [s_ref[h] for h in hs]
    y = jnp.concatenate([_dot_b(q12[h][:, :hd], s[h], _NT) + q12[h][:, hd:] for h in hs], axis=1)
    g = [_dot_b(p12[h], bh[:, cs[h]], _TN) for h in hs]
    vk = [_dot_b(v[:, cs[h]], kh[:, cs[h]], _TN) for h in hs]
    s_new = [s[h] * w_tot[:, cs[h]] + _dot_b(s[h], g[h][:hd]) + g[h][hd:] + vk[h] for h in hs]
    for h in hs:
        s_ref[h] = s_new[h]

    inv_hd = 1.0 / hd
    mean = _head_sums(y, bd_ref) * inv_hd
    yc = y - mean
    var = _head_sums(yc * yc, bd_ref) * inv_hd
    yn = yc * lax.rsqrt(var + RWKV_GN_EPS) * lng_ref[...] + lnb_ref[...]
    bonus = _head_sums(r * k * rk_ref[...], bd_ref) * v
    o_ref[0] = ((yn + bonus) * gate).astype(o_ref.dtype)


def _rwkv_branch(pb, mu, w0, w2, a0, a2, g2, k_k, k_a, r_k, lnx_g, lnx_b, out_dtype=BF16):
    bsz, seq, cols = pb.shape
    L, W = RW_L, RWKV_WIDTH
    tril = jnp.asarray(np.tril(np.ones((L, L), np.float32)), BF16)
    seg = np.arange(RW_SEG) // RWKV_HEAD
    bd = jnp.asarray((seg[:, None] == seg[None, :]).astype(np.float32), BF16)

    def vec(a):
        return a.reshape(1, -1).astype(F32)

    def full(shape):
        return pl.BlockSpec(shape, lambda b, c: (0,) * len(shape))

    return pl.pallas_call(
        _rwkv_kernel,
        grid=(bsz, seq // L),
        in_specs=[pl.BlockSpec((1, L, cols), lambda b, c: (b, c, 0)),
                  pl.BlockSpec((1, SUBLANE, cols), lambda b, c: (b, jnp.maximum(c * (L // SUBLANE) - 1, 0), 0)),
                  full((1, cols)), full((1, W)), full((DECAY_LORA, W)), full((1, W)), full((AAA_LORA, W)),
                  full((GATE_LORA, W)), full((1, W)), full((1, W)), full((1, W)), full((1, W)), full((1, W)),
                  full((L, L)), full((RW_SEG, RW_SEG))],
        out_specs=pl.BlockSpec((1, L, W), lambda b, c: (b, c, 0)),
        out_shape=jax.ShapeDtypeStruct((bsz, seq, W), out_dtype),
        scratch_shapes=[pltpu.VMEM((RWKV_HEADS, RWKV_HEAD, RWKV_HEAD), F32)],
        compiler_params=_cparams(("parallel", "arbitrary")),
        name="rwkv7_time_mix",
    )(pb, pb, vec(mu), vec(w0), w2.astype(BF16), vec(a0), a2.astype(BF16), g2.astype(BF16), vec(k_k), vec(k_a),
      vec(r_k), vec(lnx_g), vec(lnx_b), tril, bd)


def _mamba_kernel(p_ref, halo_ref, cw_ref, cb_ref, dtb_ref, alog_ref, dexp_ref, ng_ref, e_ref, triu_ref,
                  o_ref, h_ref, y_sc):
    L, hd, ns = SSM_CHUNK, SSM_HEAD_DIM, SSM_STATE
    hg = SSM_HEADS // SSM_GROUPS
    gw = hg * hd
    c = pl.program_id(1)

    @pl.when(c == 0)
    def _():
        h_ref[...] = jnp.zeros(h_ref.shape, F32)

    x0 = SSM_INNER
    xr = p_ref[0, :, x0:x0 + SSM_CONV_CH]
    halo = halo_ref[0, :, x0:x0 + SSM_CONV_CH] * jnp.where(c > 0, 1.0, 0.0)
    row8 = lax.broadcasted_iota(jnp.int32, halo.shape, 0)
    acc = xr * cw_ref[SSM_CONV - 1:SSM_CONV, :] + cb_ref[...]
    for s in range(1, SSM_CONV):
        rolled = pltpu.roll(xr, s, 0)
        top = jnp.where(row8 < s, pltpu.roll(halo, s, 0), rolled[0:SUBLANE])
        shifted = jnp.concatenate([top, rolled[SUBLANE:]], axis=0)
        acc = acc + shifted * cw_ref[SSM_CONV - 1 - s:SSM_CONV - s, :]
    xbc = acc * _sigmoid(acc)
    xs = xbc[:, :SSM_INNER]
    bm = xbc[:, SSM_INNER:SSM_INNER + SSM_GROUPS * ns]
    cm = xbc[:, SSM_INNER + SSM_GROUPS * ns:]

    dt = _softplus(p_ref[0, :, x0 + SSM_CONV_CH:x0 + SSM_CONV_CH + SSM_HEADS] + dtb_ref[...])
    a = dt * (-jnp.exp(alog_ref[...]))
    triu = triu_ref[...]
    a_cum_t = _dot_exact_rhs(_split3(a), triu, _TN)
    a_cum_e = _dot_exact_rhs(_split3(a_cum_t), e_ref[...], _TN)
    dt_e = _dot_exact_rhs(_split3(dt), e_ref[...])
    xdt = xs * dt_e
    tot = a_cum_e[L - 1:L, :]
    e_in = jnp.exp(a_cum_e)
    xw = xdt * jnp.exp(tot - a_cum_e)
    e_tot = jnp.exp(tot)
    lower = lax.broadcasted_iota(jnp.int32, (L, L), 1) <= lax.broadcasted_iota(jnp.int32, (L, L), 0)

    for g in range(SSM_GROUPS):
        gs = slice(g * gw, (g + 1) * gw)
        bg = bm[:, g * ns:(g + 1) * ns]
        cg = cm[:, g * ns:(g + 1) * ns]
        cb = _dot_b(cg, bg, _NT)
        ht = h_ref[g]
        y_off = _dot_b(cg, ht) * e_in[:, gs]
        for j in range(hg):
            h = g * hg + j
            hsl = slice(h * hd, (h + 1) * hd)
            seg = a_cum_e[:, h * hd:h * hd + 1] - a_cum_t[h:h + 1, :]
            m = cb * jnp.exp(jnp.where(lower, seg, -jnp.inf))
            y_sc[:, hsl] = _dot_b(m, xdt[:, hsl]) + y_off[:, j * hd:(j + 1) * hd]
        h_ref[g] = ht * e_tot[:, gs] + _dot_b(bg, xw[:, gs], _TN)

    z = p_ref[0, :, 0:SSM_INNER]
    y = (y_sc[...] + xs * dexp_ref[...]) * (z * _sigmoid(z))
    for g in range(SSM_GROUPS):
        gs = slice(g * gw, (g + 1) * gw)
        o_ref[0, :, gs] = _rms(y[:, gs], ng_ref[:, gs]).astype(o_ref.dtype)


def _mamba_branch(pc, conv_w, conv_b, dt_bias, a_log, d_skip, norm_g, out_dtype=BF16):
    bsz, seq, cols = pc.shape
    L = SSM_CHUNK
    triu = jnp.asarray(np.triu(np.ones((L, L), np.float32)), BF16)
    expand = jnp.asarray(np.repeat(np.eye(SSM_HEADS, dtype=np.float32), SSM_HEAD_DIM, axis=1), BF16)

    def full(shape):
        return pl.BlockSpec(shape, lambda b, c: (0,) * len(shape))

    return pl.pallas_call(
        _mamba_kernel,
        grid=(bsz, seq // L),
        in_specs=[pl.BlockSpec((1, L, cols), lambda b, c: (b, c, 0)),
                  pl.BlockSpec((1, SUBLANE, cols), lambda b, c: (b, jnp.maximum(c * (L // SUBLANE) - 1, 0), 0)),
                  full((SSM_CONV, SSM_CONV_CH)), full((1, SSM_CONV_CH)), full((1, SSM_HEADS)), full((1, SSM_HEADS)),
                  full((1, SSM_INNER)), full((1, SSM_INNER)), full((SSM_HEADS, SSM_INNER)), full((L, L))],
        out_specs=pl.BlockSpec((1, L, SSM_INNER), lambda b, c: (b, c, 0)),
        out_shape=jax.ShapeDtypeStruct((bsz, seq, SSM_INNER), out_dtype),
        scratch_shapes=[pltpu.VMEM((SSM_GROUPS, SSM_STATE, SSM_INNER // SSM_GROUPS), F32),
                        pltpu.VMEM((L, SSM_INNER), F32)],
        compiler_params=_cparams(("parallel", "arbitrary")),
        name="mamba2_mixer",
    )(pc, pc, conv_w, conv_b.reshape(1, -1), dt_bias.reshape(1, -1), a_log.reshape(1, -1),
      jnp.repeat(d_skip, SSM_HEAD_DIM).reshape(1, -1), norm_g.reshape(1, -1), expand, triu)


def _merge_kernel(oa_ref, ob_ref, oc_ref, pg_ref, x_ref, wa_ref, wb_ref, wc_ref, wo_ref, g1_ref, g2_ref,
                  xo_ref, hf_ref):
    d = D_MODEL
    merged = (_sigmoid(pg_ref[:, 0:d].astype(F32)) * _dot(oa_ref[...], wa_ref[...])
              + _sigmoid(pg_ref[:, d:2 * d].astype(F32)) * _dot(ob_ref[...], wb_ref[...])
              + _sigmoid(pg_ref[:, 2 * d:3 * d].astype(F32)) * _dot(oc_ref[...], wc_ref[...]))
    x = x_ref[...] + _rms(_dot(merged.astype(BF16), wo_ref[...]), g1_ref[...])
    xo_ref[...] = x
    hf_ref[...] = _rms(x, g2_ref[...]).astype(hf_ref.dtype)


def _merge_out(o_a, o_b, o_c, pg, x, w_o_att, w_o_rwkv, w_o_ssm, w_out, post_mix_g, pre_ffn_g, tm=256):
    n, d = x.shape

    def row(width):
        return pl.BlockSpec((tm, width), lambda i: (i, 0))

    def full(shape):
        return pl.BlockSpec(shape, lambda i: (0, 0))

    return pl.pallas_call(
        _merge_kernel,
        grid=(n // tm,),
        in_specs=[row(ATT_WIDTH), row(RWKV_WIDTH), row(SSM_INNER), row(GATE_COLS), row(d),
                  full((ATT_WIDTH, d)), full((RWKV_WIDTH, d)), full((SSM_INNER, d)), full((d, d)),
                  full((1, d)), full((1, d))],
        out_specs=[row(d), row(d)],
        out_shape=[jax.ShapeDtypeStruct((n, d), F32), jax.ShapeDtypeStruct((n, d), BF16)],
        compiler_params=_cparams(("parallel",)),
        name="merge_out",
    )(o_a, o_b, o_c, pg, x, w_o_att.astype(BF16), w_o_rwkv.astype(BF16), w_o_ssm.astype(BF16),
      w_out.astype(BF16), post_mix_g.reshape(1, d), pre_ffn_g.reshape(1, d))


W_IN_PIECES = (
    (0, 2 * ATT_WIDTH, 2 * ATT_WIDTH),
    (2 * ATT_WIDTH, ATT_WIDTH, ATT_WIDTH),
    (3 * ATT_WIDTH, A_COLS - 3 * ATT_WIDTH, IDX_COLS),
    (A_COLS, B_COLS, B_COLS),
    (A_COLS + B_COLS, C_COLS, C_COLS_PAD),
    (A_COLS + B_COLS + C_COLS, GATE_COLS, GATE_COLS),
)
W_SPLIT_ROWS = 128


def _split_w_kernel(w_ref, *out_refs):
    for (start, width, padded), o_ref in zip(W_IN_PIECES, out_refs):
        piece = w_ref[:, start:start + width].astype(o_ref.dtype)
        if padded > width:
            piece = jnp.concatenate([piece, jnp.zeros((piece.shape[0], padded - width), o_ref.dtype)], axis=1)
        o_ref[...] = piece


def _split_w_in(w_in):
    k, cols = w_in.shape
    tk = W_SPLIT_ROWS
    return pl.pallas_call(
        _split_w_kernel,
        grid=(k // tk,),
        in_specs=[pl.BlockSpec((tk, cols), lambda i: (i, 0))],
        out_specs=[pl.BlockSpec((tk, p[2]), lambda i: (i, 0)) for p in W_IN_PIECES],
        out_shape=[jax.ShapeDtypeStruct((k, p[2]), BF16) for p in W_IN_PIECES],
        compiler_params=_cparams(("parallel",)),
        name="split_w_in",
    )(w_in)


def _layer(x, h, bsz, seq, g_next, w_in, w_o_att, rwkv_mu, rwkv_w0, rwkv_w2, rwkv_a0, rwkv_a2, rwkv_g2,
           rwkv_k_k, rwkv_k_a, rwkv_r_k, rwkv_lnx_g, rwkv_lnx_b, w_o_rwkv, ssm_conv_w, ssm_conv_b,
           ssm_dt_bias, ssm_a_log, ssm_d, ssm_norm_g, w_o_ssm, w_out, post_mix_g, pre_ffn_g,
           w_ff1, w_ff2, post_ffn_g):
    n = bsz * seq
    topk = min(TOPK_MAX, seq // 4)
    w_qk, w_v, w_idx, w_b, w_c, w_g = _split_w_in(w_in)

    q_scale = ATT_HEAD_DIM ** -0.5 * math.log2(math.e)
    tq_ = _rope_tables(seq, ATT_HEAD_DIM, 1, ATT_ROT, LANE, q_scale)
    tk_ = _rope_tables(seq, ATT_HEAD_DIM, 1, ATT_ROT, LANE)
    qk = _mm_rope(h, w_qk, tuple(jnp.stack([a, b]) for a, b in zip(tq_, tk_)), ATT_ROT // 2,
                  BF16, ATT_WIDTH, seq).reshape(bsz, seq, 2 * ATT_WIDTH)
    vt = _mm_t(w_v.T, h)
    ti = list(_rope_tables(seq, IDX_HEAD_DIM, IDX_HEADS + 1, IDX_ROT, IDX_COLS))
    wi0 = IDX_Q_WIDTH + IDX_HEAD_DIM
    ti[0] = ti[0].at[:, wi0:wi0 + IDX_HEADS].set(IDX_HEADS ** -0.5)
    idx = _mm_rope(h, w_idx, tuple(t[None] for t in ti), IDX_ROT // 2,
                   F32, IDX_COLS, seq).reshape(bsz, seq, IDX_COLS)
    mask = _dsa_mask(idx, topk)
    o_a = _dsa_attend(qk, vt, mask).reshape(n, ATT_WIDTH)

    pb = _mm(h, w_b, F32).reshape(bsz, seq, B_COLS)
    o_b = _rwkv_branch(pb, rwkv_mu, rwkv_w0, rwkv_w2, rwkv_a0, rwkv_a2, rwkv_g2, rwkv_k_k, rwkv_k_a, rwkv_r_k,
                       rwkv_lnx_g, rwkv_lnx_b).reshape(n, RWKV_WIDTH)
    pc = _mm(h, w_c, F32).reshape(bsz, seq, C_COLS_PAD)
    o_c = _mamba_branch(pc, ssm_conv_w, ssm_conv_b, ssm_dt_bias, ssm_a_log, ssm_d, ssm_norm_g).reshape(n, SSM_INNER)

    pg = _mm(h, w_g, BF16)
    x, hf = _merge_out(o_a, o_b, o_c, pg, x, w_o_att, w_o_rwkv, w_o_ssm, w_out, post_mix_g, pre_ffn_g)
    f = _mm(hf, w_ff1, out_dtype=BF16, act="relu2")
    return _mm_norm_res(f, w_ff2, post_ffn_g, x, g_next)


def kernel(x, pre_mix_g, w_in, w_o_att, rwkv_mu, rwkv_w0, rwkv_w2, rwkv_a0, rwkv_a2, rwkv_g2, rwkv_k_k,
           rwkv_k_a, rwkv_r_k, rwkv_lnx_g, rwkv_lnx_b, w_o_rwkv, ssm_conv_w, ssm_conv_b, ssm_dt_bias,
           ssm_a_log, ssm_d, ssm_norm_g, w_o_ssm, w_out, post_mix_g, pre_ffn_g, w_ff1, w_ff2, post_ffn_g):
    bsz, seq, d = x.shape
    depth = pre_mix_g.shape[0]
    params = (w_in, w_o_att, rwkv_mu, rwkv_w0, rwkv_w2, rwkv_a0, rwkv_a2, rwkv_g2, rwkv_k_k,
              rwkv_k_a, rwkv_r_k, rwkv_lnx_g, rwkv_lnx_b, w_o_rwkv, ssm_conv_w, ssm_conv_b, ssm_dt_bias,
              ssm_a_log, ssm_d, ssm_norm_g, w_o_ssm, w_out, post_mix_g, pre_ffn_g, w_ff1, w_ff2, post_ffn_g)
    y = x.reshape(bsz * seq, d)
    h = _rmsnorm(y, pre_mix_g[0])
    for i in range(depth):
        g_next = pre_mix_g[min(i + 1, depth - 1)]
        y, h = _layer(y, h, bsz, seq, g_next, *[p[i] for p in params])
    return y.reshape(bsz, seq, d)
```
